```python
import math
import jax, jax.numpy as jnp
from jax import lax
import numpy as np

D_MODEL = 1024
BATCH = 2
SEQ = 16384
DEPTH = 2
DEC_BATCH = 2
DEC_SEQ = 8192
PAST_LEN = 128

D_MLSTM = D_MODEL // 2
MLSTM_HEADS = 4
MLSTM_HEAD_DIM = D_MLSTM // MLSTM_HEADS
CHUNK = 128
D_HYENA = D_MODEL // 2
HYENA_ORDER = 2
HYENA_EMB = 33
HYENA_BANDS = (HYENA_EMB - 1) // 2
HYENA_FILTER_HIDDEN = 64
HYENA_DECAY_TARGET = 1e-2
HYENA_FAST_DECAY = 0.3
HYENA_SLOW_DECAY = 1.5
SHORT_CONV = 3
N_BRANCH = 2
D_FF = 4 * D_MODEL
EPS = 1e-6

IN_SIZES = (D_MLSTM, D_MLSTM, D_MLSTM, D_MLSTM,
            MLSTM_HEADS, MLSTM_HEADS, MLSTM_HEADS, MLSTM_HEADS,
            D_HYENA, D_HYENA, D_HYENA,
            D_MODEL, D_MODEL)
N_IN = 4 * D_MLSTM + 4 * MLSTM_HEADS + 3 * D_HYENA + N_BRANCH * D_MODEL
OFF_FG_FWD = 4 * D_MLSTM + MLSTM_HEADS
OFF_FG_BWD = 4 * D_MLSTM + 3 * MLSTM_HEADS

kernel_name = 'hybrid_mlstm_hyena_encoder'


def rmsnorm(x, g):
    xf = x.astype(jnp.float32)
    y = xf * lax.rsqrt(jnp.mean(xf * xf, axis=-1, keepdims=True) + EPS)
    return (y * g.astype(jnp.float32)).astype(x.dtype)


def split_last(t, sizes):
    outs, start = [], 0
    for s in sizes:
        outs.append(t[..., start:start + s])
        start += s
    return outs


def mlstm_chunkwise(q, k, v, i_pre, f_pre):
    B, S, NH, DH = q.shape
    NC = S // CHUNK

    def to_chunks(t):
        t = t.reshape((B, NC, CHUNK, NH) + t.shape[3:])
        return jnp.moveaxis(t, 3, 1)

    qc, kc, vc = to_chunks(q), to_chunks(k), to_chunks(v)
    log_i = to_chunks(i_pre.astype(jnp.float32))
    log_f = jax.nn.log_sigmoid(to_chunks(f_pre.astype(jnp.float32)))
    b = jnp.cumsum(log_f, axis=-1)
    g = b[..., -1]

    a = g[..., None] - b + log_i
    m_loc = jnp.max(a, axis=-1)
    w_loc = jnp.exp(a - m_loc[..., None])
    kv_loc = jnp.einsum('bhcld,bhcle->bhcde', w_loc[..., None] * kc, vc)
    n_loc = jnp.einsum('bhcl,bhcld->bhcd', w_loc, kc)

    def step(carry, inp):
        C, n, m = carry
        g_c, m_l, kv_c, n_c = inp
        m_new = jnp.maximum(g_c + m, m_l)
        s_old = jnp.exp(g_c + m - m_new)
        s_new = jnp.exp(m_l - m_new)
        C_new = s_old[..., None, None] * C + s_new[..., None, None] * kv_c
        n_new = s_old[..., None] * n + s_new[..., None] * n_c
        return (C_new, n_new, m_new), (C, n, m)

    init = (jnp.zeros((B, NH, DH, DH), jnp.float32),
            jnp.zeros((B, NH, DH), jnp.float32),
            jnp.zeros((B, NH), jnp.float32))
    xs = (jnp.moveaxis(g, 2, 0), jnp.moveaxis(m_loc, 2, 0),
          jnp.moveaxis(kv_loc, 2, 0), jnp.moveaxis(n_loc, 2, 0))
    _, (C_prev, n_prev, m_prev) = lax.scan(step, init, xs)
    C_prev = jnp.moveaxis(C_prev, 0, 2)
    n_prev = jnp.moveaxis(n_prev, 0, 2)
    m_prev = jnp.moveaxis(m_prev, 0, 2)

    lower = jnp.tril(jnp.ones((CHUNK, CHUNK), dtype=bool))
    log_D = jnp.where(lower, b[..., :, None] - b[..., None, :] + log_i[..., None, :], -jnp.inf)
    log_inter = b + m_prev[..., None]
    m_t = jnp.maximum(log_inter, jnp.max(log_D, axis=-1))
    D = jnp.exp(log_D - m_t[..., None])
    s_inter = jnp.exp(log_inter - m_t)
    scores = jnp.einsum('bhcld,bhcsd->bhcls', qc, kc) * D
    num = (jnp.einsum('bhcls,bhcse->bhcle', scores, vc)
           + s_inter[..., None] * jnp.einsum('bhcld,bhcde->bhcle', qc, C_prev))
    den = jnp.sum(scores, axis=-1) + s_inter * jnp.einsum('bhcld,bhcd->bhcl', qc, n_prev)
    h = num / jnp.maximum(jnp.abs(den), jnp.exp(-m_t))[..., None]
    return jnp.moveaxis(h, 1, 3).reshape(B, S, NH, DH)


def mlstm_branch(q, k, v, o, ig_f, fg_f, ig_b, fg_b, head_norm):
    B, S, _ = q.shape

    def heads(t):
        return t.astype(jnp.float32).reshape(B, S, MLSTM_HEADS, MLSTM_HEAD_DIM)

    qh = heads(q) * (MLSTM_HEAD_DIM ** -0.5)
    kh, vh = heads(k), heads(v)
    h_fwd = mlstm_chunkwise(qh, kh, vh, ig_f, fg_f)

    def rev(t):
        return jnp.flip(t, axis=1)

    h_bwd = rev(mlstm_chunkwise(rev(qh), rev(kh), rev(vh), rev(ig_b), rev(fg_b)))
    h = h_fwd + h_bwd
    mu = jnp.mean(h, axis=-1, keepdims=True)
    var = jnp.mean(jnp.square(h - mu), axis=-1, keepdims=True)
    h = ((h - mu) * lax.rsqrt(var + EPS)).reshape(B, S, D_MLSTM)
    h = h * head_norm.astype(jnp.float32) * jax.nn.sigmoid(o.astype(jnp.float32))
    return h.astype(q.dtype)


def hyena_positions(L):
    t = jnp.linspace(0.0, 1.0, L, dtype=jnp.float32)[:, None]
    w = (2.0 * math.pi) * jnp.arange(L, dtype=jnp.float32)[:, None] / L
    f = jnp.linspace(1e-4, HYENA_BANDS - 1, HYENA_BANDS, dtype=jnp.float32)[None, :]
    z = jnp.concatenate([t, jnp.cos(f * w), -jnp.sin(f * w)], axis=-1)
    return z, t


def hyena_filter_freq(L, w1, b1, w2, b2, w3, b3, freq, w_fo, decay):
    f32 = jnp.float32
    z, t = hyena_positions(L)
    fr = freq.astype(f32)
    h = jnp.sin(fr * (z @ w1.astype(f32) + b1.astype(f32)))
    h = jnp.sin(fr * (h @ w2.astype(f32) + b2.astype(f32)))
    h = jnp.sin(fr * (h @ w3.astype(f32) + b3.astype(f32)))
    h = (h @ w_fo.astype(f32)).reshape(L, 2, D_HYENA)
    h = h * jnp.exp(-t[:, :, None] * jnp.abs(decay.astype(f32)))
    h = h / jnp.sum(jnp.abs(h), axis=(0, 1), keepdims=True)
    kern = jnp.concatenate([h[:, 0], jnp.zeros((1, D_HYENA), f32), h[:0:-1, 1]], axis=0)
    return jnp.fft.rfft(kern, axis=0)


def hyena_branch(u, conv_w, conv_b, skip, k_freq):
    L = u.shape[1]
    uf = u.astype(jnp.float32)
    cw = conv_w.astype(jnp.float32)
    up = jnp.pad(uf, ((0, 0), (1, 1), (0, 0)))
    uf = up[:, :-2] * cw[0] + up[:, 1:-1] * cw[1] + up[:, 2:] * cw[2] + conv_b.astype(jnp.float32)
    x0, x1, v = split_last(uf, (D_HYENA, D_HYENA, D_HYENA))
    zc = x1 * v
    zf = jnp.fft.rfft(zc, n=2 * L, axis=1)
    y = jnp.fft.irfft(zf * k_freq[None], n=2 * L, axis=1)[:, :L]
    y = y + skip.astype(jnp.float32) * zc
    return (x0 * y).astype(u.dtype)


def trunk(x, norm_mix, w_in, b_in, mlstm_norm, hy_conv_w, hy_conv_b, hy_w1, hy_b1,
          hy_w2, hy_b2, hy_w3, hy_b3, hy_freq, hy_w_fo, hy_decay, hy_skip,
          w_branch_a, w_branch_b, w_out, norm_mlp, w_mlp1, w_mlp2, norm_final):
    L = x.shape[1]
    for l in range(DEPTH):
        n = rmsnorm(x, norm_mix[l])
        p = n @ w_in[l] + b_in[l]
        (q, k, v, o, ig_f, fg_f, ig_b, fg_b,
         hx0, hx1, hv, gate_a, gate_b) = split_last(p, IN_SIZES)
        a_out = mlstm_branch(q, k, v, o, ig_f, fg_f, ig_b, fg_b, mlstm_norm[l])
        k_freq = hyena_filter_freq(L, hy_w1[l], hy_b1[l], hy_w2[l], hy_b2[l], hy_w3[l],
                                   hy_b3[l], hy_freq[l], hy_w_fo[l], hy_decay[l])
        b_out = hyena_branch(jnp.concatenate([hx0, hx1, hv], axis=-1),
                             hy_conv_w[l], hy_conv_b[l], hy_skip[l], k_freq)
        merged = (jax.nn.sigmoid(gate_a) * (a_out @ w_branch_a[l])
                  + jax.nn.sigmoid(gate_b) * (b_out @ w_branch_b[l]))
        x = x + merged @ w_out[l]
        n = rmsnorm(x, norm_mlp[l])
        x = x + jnp.square(jax.nn.relu(n @ w_mlp1[l])) @ w_mlp2[l]
    return rmsnorm(x, norm_final)


def setup_inputs(seed: int = 0) -> dict:
    key = jax.random.key(seed)
    ks = jax.random.split(key, 26)
    f32 = jnp.float32

    def nrm(k, shape, s):
        return s * jax.random.normal(k, shape, f32)

    f_bias = jnp.linspace(3.0, 6.0, MLSTM_HEADS, dtype=f32)
    b_in = nrm(ks[4], (DEPTH, N_IN), 0.02)
    b_in = b_in.at[:, OFF_FG_FWD:OFF_FG_FWD + MLSTM_HEADS].add(f_bias)
    b_in = b_in.at[:, OFF_FG_BWD:OFF_FG_BWD + MLSTM_HEADS].add(f_bias)
    decay0 = jnp.linspace(math.log(HYENA_DECAY_TARGET) / HYENA_SLOW_DECAY,
                          math.log(HYENA_DECAY_TARGET) / HYENA_FAST_DECAY, D_HYENA, dtype=f32)
    H = HYENA_FILTER_HIDDEN
    return {
        'x_prompt': nrm(ks[0], (BATCH, SEQ, D_MODEL), 1.0),
        'x_sample': nrm(ks[1], (DEC_BATCH, DEC_SEQ, D_MODEL), 1.0),
        'norm_mix': 1.0 + nrm(ks[2], (DEPTH, D_MODEL), 0.02),
        'w_in': nrm(ks[3], (DEPTH, D_MODEL, N_IN), D_MODEL ** -0.5),
        'b_in': b_in,
        'mlstm_norm': 1.0 + nrm(ks[5], (DEPTH, D_MLSTM), 0.02),
        'hy_conv_w': nrm(ks[6], (DEPTH, SHORT_CONV, 3 * D_HYENA), SHORT_CONV ** -0.5),
        'hy_conv_b': nrm(ks[7], (DEPTH, 3 * D_HYENA), 0.02),
        'hy_w1': nrm(ks[8], (DEPTH, HYENA_EMB, H), HYENA_EMB ** -0.5),
        'hy_b1': nrm(ks[9], (DEPTH, H), 0.1),
        'hy_w2': nrm(ks[10], (DEPTH, H, H), H ** -0.5),
        'hy_b2': nrm(ks[11], (DEPTH, H), 0.1),
        'hy_w3': nrm(ks[12], (DEPTH, H, H), H ** -0.5),
        'hy_b3': nrm(ks[13], (DEPTH, H), 0.1),
        'hy_freq': 1.0 + nrm(ks[14], (DEPTH, H), 0.02),
        'hy_w_fo': nrm(ks[15], (DEPTH, H, 2 * D_HYENA), H ** -0.5),
        'hy_decay': decay0[None, None, :] + nrm(ks[16], (DEPTH, 2, D_HYENA), 0.1),
        'hy_skip': nrm(ks[17], (DEPTH, D_HYENA), 0.1),
        'w_branch_a': nrm(ks[18], (DEPTH, D_MLSTM, D_MODEL), D_MLSTM ** -0.5),
        'w_branch_b': nrm(ks[19], (DEPTH, D_HYENA, D_MODEL), D_HYENA ** -0.5),
        'w_out': nrm(ks[20], (DEPTH, D_MODEL, D_MODEL), D_MODEL ** -0.5),
        'norm_mlp': 1.0 + nrm(ks[21], (DEPTH, D_MODEL), 0.02),
        'w_mlp1': nrm(ks[22], (DEPTH, D_MODEL, D_FF), D_MODEL ** -0.5),
        'w_mlp2': nrm(ks[23], (DEPTH, D_FF, D_MODEL), D_FF ** -0.5),
        'norm_final': 1.0 + nrm(ks[24], (D_MODEL,), 0.02),
    }


def reference(x_prompt, x_sample, norm_mix, w_in, b_in, mlstm_norm, hy_conv_w, hy_conv_b,
              hy_w1, hy_b1, hy_w2, hy_b2, hy_w3, hy_b3, hy_freq, hy_w_fo, hy_decay, hy_skip,
              w_branch_a, w_branch_b, w_out, norm_mlp, w_mlp1, w_mlp2, norm_final):
    y_prompt = trunk(x_prompt, norm_mix, w_in, b_in, mlstm_norm, hy_conv_w, hy_conv_b,
                     hy_w1, hy_b1, hy_w2, hy_b2, hy_w3, hy_b3, hy_freq, hy_w_fo, hy_decay,
                     hy_skip, w_branch_a, w_branch_b, w_out, norm_mlp, w_mlp1, w_mlp2,
                     norm_final)
    y_sample = trunk(x_sample, norm_mix, w_in, b_in, mlstm_norm, hy_conv_w, hy_conv_b,
                     hy_w1, hy_b1, hy_w2, hy_b2, hy_w3, hy_b3, hy_freq, hy_w_fo, hy_decay,
                     hy_skip, w_branch_a, w_branch_b, w_out, norm_mlp, w_mlp1, w_mlp2,
                     norm_final)
    return (y_prompt, y_sample)
```

```python
import functools
import math

import numpy as np
import jax
import jax.numpy as jnp
from jax import lax
from jax.experimental import pallas as pl
from jax.experimental.pallas import tpu as pltpu

F32 = jnp.float32
BF16 = jnp.bfloat16

EPS = 1e-6
MLSTM_HEADS = 4
HEAD_DIM = 128
CHUNK = 128
HYENA_BANDS = 16
FEAT_PAD = 128
DFT_N1 = 128
TOKEN_BLOCK = 512
VMEM_LIMIT = 48 * 1024 * 1024
HIGHEST = lax.Precision.HIGHEST


def _params(*sem):
    return pltpu.CompilerParams(dimension_semantics=sem, vmem_limit_bytes=VMEM_LIMIT)


def _rms(x, g):
    return x * lax.rsqrt(jnp.mean(x * x, axis=-1, keepdims=True) + EPS) * g


def _dot(a, b):
    return jnp.dot(a, b, preferred_element_type=F32)


def _inproj_kernel(x_ref, xp_ref, xn_ref, g_ref, wm_ref, bm_ref, wg_ref, bg_ref,
                   wu_ref, bu_ref, cw_ref, cb_ref,
                   q_ref, k_ref, v_ref, o_ref, gt_ref, x0_ref, zc_ref):
    i = pl.program_id(1)
    last = pl.num_programs(1) - 1
    g = g_ref[...]
    n = _rms(x_ref[0], g).astype(BF16)
    dm = HEAD_DIM * MLSTM_HEADS
    pm = _dot(n, wm_ref[...]) + bm_ref[...]
    q_ref[0] = (pm[:, :dm] * (HEAD_DIM ** -0.5)).astype(BF16)
    k_ref[0] = pm[:, dm:2 * dm].astype(BF16)
    v_ref[0] = pm[:, 2 * dm:3 * dm].astype(BF16)
    o_ref[0] = pm[:, 3 * dm:]
    gt_ref[0] = _dot(n, wg_ref[...]) + bg_ref[...]

    u = _dot(n, wu_ref[...]) + bu_ref[...]
    halo = jnp.concatenate([xp_ref[0], xn_ref[0]], axis=0)
    uh = _dot(_rms(halo, g).astype(BF16), wu_ref[...]) + bu_ref[...]
    u_prev = jnp.where(i > 0, uh[7:8], 0.0)
    u_next = jnp.where(i < last, uh[8:9], 0.0)
    tm = u.shape[0]
    row = lax.broadcasted_iota(jnp.int32, (tm, 1), 0)
    u_m1 = jnp.where(row == 0, u_prev, pltpu.roll(u, 1, 0))
    u_p1 = jnp.where(row == tm - 1, u_next, pltpu.roll(u, tm - 1, 0))
    cw = cw_ref[...]
    c = u_m1 * cw[0:1] + u * cw[1:2] + u_p1 * cw[2:3] + cb_ref[...]
    dh = c.shape[1] // 3
    x0_ref[0] = c[:, :dh]
    zc_ref[0] = c[:, dh:2 * dh] * c[:, 2 * dh:]


def _inproj(x, g, wm, bm, wg, bg, wu, bu, cw, cb):
    B, L, D = x.shape
    tm = min(TOKEN_BLOCK, L)
    nb = L // tm
    r8 = tm // 8
    dm = wm.shape[1] // 4
    dh = wu.shape[1] // 3
    ng = wg.shape[1]
    full = lambda a: pl.BlockSpec(a.shape, lambda b, i: (0,) * a.ndim)
    tok = lambda w: pl.BlockSpec((1, tm, w), lambda b, i: (b, i, 0))
    in_specs = [
        tok(D),
        pl.BlockSpec((1, 8, D), lambda b, i: (b, jnp.maximum(i * r8 - 1, 0), 0)),
        pl.BlockSpec((1, 8, D), lambda b, i: (b, jnp.minimum((i + 1) * r8, L // 8 - 1), 0)),
        full(g), full(wm), full(bm), full(wg), full(bg), full(wu), full(bu), full(cw), full(cb),
    ]
    out_shape = (
        jax.ShapeDtypeStruct((B, L, dm), BF16), jax.ShapeDtypeStruct((B, L, dm), BF16),
        jax.ShapeDtypeStruct((B, L, dm), BF16), jax.ShapeDtypeStruct((B, L, dm), F32),
        jax.ShapeDtypeStruct((B, L, ng), F32),
        jax.ShapeDtypeStruct((B, L, dh), F32), jax.ShapeDtypeStruct((B, L, dh), F32),
    )
    out_specs = (tok(dm), tok(dm), tok(dm), tok(dm), tok(ng), tok(dh), tok(dh))
    return pl.pallas_call(
        _inproj_kernel, grid=(B, nb), in_specs=in_specs, out_specs=out_specs,
        out_shape=out_shape, compiler_params=_params("parallel", "parallel"),
        name="inproj")(x, x, x, g, wm, bm, wg, bg, wu, bu, cw, cb)


def _log_sigmoid(x):
    return jnp.minimum(x, 0.0) - jnp.log1p(jnp.exp(-jnp.abs(x)))


def _mlstm_kernel(reverse, finish, *refs):
    if finish:
        (q_ref, k_ref, v_ref, gc_ref, gr_ref, hf_ref, o_ref, hn_ref, out_ref,
         c_s, n_s, m_s) = refs
    else:
        q_ref, k_ref, v_ref, gc_ref, gr_ref, out_ref, c_s, n_s, m_s = refs

    @pl.when(pl.program_id(1) == 0)
    def _():
        c_s[...] = jnp.zeros_like(c_s)
        n_s[...] = jnp.zeros_like(n_s)
        m_s[...] = jnp.zeros_like(m_s)

    nh = MLSTM_HEADS
    off = 2 * nh if reverse else 0
    gc = gc_ref[0]
    gr = gr_ref[0]
    ri = lax.broadcasted_iota(jnp.int32, (CHUNK, CHUNK), 0)
    ci = lax.broadcasted_iota(jnp.int32, (CHUNK, CHUNK), 1)
    lower = (ci <= ri).astype(F32)
    upper = (ri <= ci).astype(F32)
    lf_col = _log_sigmoid(gc[:, off + nh:off + 2 * nh])
    lf_row = _log_sigmoid(gr[off + nh:off + 2 * nh, :])
    if reverse:
        mask = ri <= ci
        b_col4 = jnp.dot(upper, lf_col, precision=HIGHEST, preferred_element_type=F32)
        b_row4 = jnp.dot(lf_row, lower, precision=HIGHEST, preferred_element_type=F32)
        last = 0
    else:
        mask = ci <= ri
        b_col4 = jnp.dot(lower, lf_col, precision=HIGHEST, preferred_element_type=F32)
        b_row4 = jnp.dot(lf_row, upper, precision=HIGHEST, preferred_element_type=F32)
        last = CHUNK - 1

    for h in range(nh):
        hs = slice(h * HEAD_DIM, (h + 1) * HEAD_DIM)
        li_col = gc[:, off + h:off + h + 1]
        li_row = gr[off + h:off + h + 1, :]
        b_col = b_col4[:, h:h + 1]
        b_row = b_row4[h:h + 1, :]
        g = b_row[:, last:last + 1]
        m_prev = m_s[h:h + 1, 0:1]
        a_col = g - b_col + li_col
        m_loc = jnp.max(a_col, axis=0, keepdims=True)
        w_col = jnp.exp(a_col - m_loc)
        log_d = jnp.where(mask, b_col + (li_row - b_row), -jnp.inf)
        m_t = jnp.maximum(b_col + m_prev, jnp.max(log_d, axis=1, keepdims=True))
        d_mat = jnp.exp(log_d - m_t)
        s_inter = jnp.exp(b_col + m_prev - m_t)

        q = q_ref[0, :, hs]
        k = k_ref[0, :, hs]
        v = v_ref[0, :, hs]
        scores = lax.dot_general(q, k, (((1,), (1,)), ((), ())),
                                 preferred_element_type=F32) * d_mat
        c_prev = c_s[h]
        n_prev = n_s[h:h + 1, :]
        num = _dot(scores.astype(BF16), v) + s_inter * _dot(q, c_prev.astype(BF16))
        den = (jnp.sum(scores, axis=1, keepdims=True)
               + s_inter * jnp.sum(q.astype(F32) * n_prev, axis=1, keepdims=True))
        hh = num / jnp.maximum(jnp.abs(den), jnp.exp(-m_t))

        m_new = jnp.maximum(g + m_prev, m_loc)
        s_old = jnp.exp(g + m_prev - m_new)
        s_new = jnp.exp(m_loc - m_new)
        kw = w_col * k.astype(F32)
        kv = lax.dot_general(kw.astype(BF16), v, (((0,), (0,)), ((), ())),
                             preferred_element_type=F32)
        c_s[h] = s_old * c_prev + s_new * kv
        n_s[h:h + 1, :] = s_old * n_prev + s_new * jnp.sum(kw, axis=0, keepdims=True)
        m_s[h:h + 1, :] = jnp.broadcast_to(m_new, (1, m_s.shape[1]))

        if finish:
            hs_sum = hh + hf_ref[0, :, hs]
            mu = jnp.mean(hs_sum, axis=1, keepdims=True)
            dv = hs_sum - mu
            var = jnp.mean(dv * dv, axis=1, keepdims=True)
            y = dv * lax.rsqrt(var + EPS) * hn_ref[:, hs] * jax.nn.sigmoid(o_ref[0, :, hs])
            out_ref[0, :, hs] = y.astype(out_ref.dtype)
        else:
            out_ref[0, :, hs] = hh


def _mlstm_pass(q, k, v, g_col, g_row, reverse, h_fwd=None, o=None, head_norm=None):
    B, L, dm = q.shape
    nc = L // CHUNK
    finish = h_fwd is not None
    cidx = (lambda c: nc - 1 - c) if reverse else (lambda c: c)
    tok = lambda w: pl.BlockSpec((1, CHUNK, w), lambda b, c: (b, cidx(c), 0))
    in_specs = [tok(dm), tok(dm), tok(dm), tok(g_col.shape[2]),
                pl.BlockSpec((1, g_row.shape[1], CHUNK), lambda b, c: (b, 0, cidx(c)))]
    args = [q, k, v, g_col, g_row]
    if finish:
        in_specs += [tok(dm), tok(dm), pl.BlockSpec(head_norm.shape, lambda b, c: (0, 0))]
        args += [h_fwd, o, head_norm]
    return pl.pallas_call(
        functools.partial(_mlstm_kernel, reverse, finish),
        grid=(B, nc), in_specs=in_specs, out_specs=tok(dm),
        out_shape=jax.ShapeDtypeStruct((B, L, dm), BF16 if finish else F32),
        scratch_shapes=[pltpu.VMEM((MLSTM_HEADS, HEAD_DIM, HEAD_DIM), F32),
                        pltpu.VMEM((8, HEAD_DIM), F32), pltpu.VMEM((8, 128), F32)],
        compiler_params=_params("parallel", "arbitrary"),
        name="mlstm_bwd" if reverse else "mlstm_fwd")(*args)


def _filter_kernel(seq_len, fvec_ref, w1_ref, b1_ref, w2_ref, b2_ref, w3_ref, b3_ref,
                   fr_ref, wfo_ref, dec_ref, kern_ref, l1_ref):
    i = pl.program_id(0)
    rows = kern_ref.shape[0]
    L = seq_len
    n = i * rows + lax.broadcasted_iota(jnp.int32, (rows, 1), 0)
    j = jnp.where(n < L, n, jnp.where(n == L, 0, 2 * L - n)).astype(F32)
    t = j / (L - 1)
    w = ((2.0 * math.pi) * j) / L
    lane = lax.broadcasted_iota(jnp.int32, (rows, FEAT_PAD), 1)
    ang = fvec_ref[...] * w
    z = jnp.where(lane == 0, t,
                  jnp.where(lane <= HYENA_BANDS, jnp.cos(ang),
                            jnp.where(lane <= 2 * HYENA_BANDS, -jnp.sin(ang), 0.0)))
    fr = fr_ref[...]
    hd = lambda a, b: jnp.dot(a, b, precision=HIGHEST, preferred_element_type=F32)
    h = jnp.sin(fr * (hd(z, w1_ref[...]) + b1_ref[...]))
    h = jnp.sin(fr * (hd(h, w2_ref[...]) + b2_ref[...]))
    h = jnp.sin(fr * (hd(h, w3_ref[...]) + b3_ref[...]))
    h = hd(h, wfo_ref[...]) * jnp.exp(-t * jnp.abs(dec_ref[0]))

    @pl.when(i == 0)
    def _():
        l1_ref[...] = jnp.zeros_like(l1_ref)

    l1_ref[...] += jnp.sum(jnp.abs(h), axis=0, keepdims=True)
    kern_ref[...] = jnp.where(n == L, 0.0, h)


def _filter_taps(L, fvec, w1p, b1, w2, b2, w3, b3, fr, wfo, dec):
    C = wfo.shape[1] // 2
    rows = min(256, L)
    nblk = 2 * L // rows
    full = lambda a: pl.BlockSpec(a.shape, lambda i: (0,) * a.ndim)
    dirn = lambda i: (i * rows) // L
    in_specs = [full(fvec), full(w1p), full(b1), full(w2), full(b2), full(w3), full(b3), full(fr),
                pl.BlockSpec((wfo.shape[0], C), lambda i: (0, dirn(i))),
                pl.BlockSpec((1, 1, C), lambda i: (dirn(i), 0, 0))]
    return pl.pallas_call(
        functools.partial(_filter_kernel, L),
        grid=(nblk,), in_specs=in_specs,
        out_specs=(pl.BlockSpec((rows, C), lambda i: (i, 0)), pl.BlockSpec((1, C), lambda i: (0, 0))),
        out_shape=(jax.ShapeDtypeStruct((2 * L, C), F32), jax.ShapeDtypeStruct((1, C), F32)),
        compiler_params=_params("arbitrary"),
        name="hyena_filter")(fvec, w1p, b1, w2, b2, w3, b3, fr, wfo, dec.reshape(2, 1, C))


@functools.lru_cache(maxsize=None)
def _dft_tables(n1, n2):
    n = n1 * n2
    h = n2 // 2
    a2 = 2.0 * np.pi * np.outer(np.arange(n2), np.arange(n2)) / n2
    c2, s2 = np.cos(a2), np.sin(a2)
    m1d = np.empty((n2, 2, n2))
    m1d[:, 0, :h], m1d[:, 0, h:] = c2[:, :h], s2[:, :h]
    m1d[:, 1, :h], m1d[:, 1, h:] = -s2[:, :h], c2[:, :h]
    m1f = np.stack([c2, -s2], axis=1)
    a1 = 2.0 * np.pi * np.outer(np.arange(n1), np.arange(n1)) / n1
    f1r, f1i = np.cos(a1), -np.sin(a1)
    pg = np.block([[f1r, -f1i], [f1i, f1r]])
    pgs = np.block([[-f1i, -f1r], [f1r, -f1i]])
    qd = np.block([[f1r, f1i], [-f1i, f1r]])
    at = 2.0 * np.pi * np.outer(np.arange(n2), np.arange(n1)) / n
    tr, ti = np.cos(at), -np.sin(at)
    txr = np.concatenate([tr, tr], axis=1)
    txi = np.concatenate([ti, ti], axis=1)
    er, ei = c2[:h] / n, s2[:h] / n
    p1 = np.stack([er, -ei], axis=2).reshape(h, 2 * n2)
    p3 = np.stack([ei, er], axis=2).reshape(h, 2 * n2)
    pp = np.concatenate([p1, p3], axis=0)
    pps = np.concatenate([-p3, p1], axis=0)
    uxr = np.repeat(np.cos(at).T, 2, axis=1)
    uxi = np.repeat(np.sin(at).T, 2, axis=1)
    f32 = lambda a: np.asarray(a, np.float32)
    return dict(m1d=f32(m1d.reshape(2 * n2, n2)), m1f=f32(m1f.reshape(2 * n2, n2)),
                pg=f32(pg), pgs=f32(pgs), qd=f32(qd), txr=f32(txr), txi=f32(txi),
                pp=f32(pp), pps=f32(pps), uxr=f32(uxr), uxi=f32(uxi))


def _dft1_kernel(m_ref, *refs):
    out_ref = refs[-1]
    xs = [r[0] for r in refs[:-1]]
    x = xs[0] if len(xs) == 1 else jnp.concatenate(xs, axis=0)
    out_ref[...] = _dot(m_ref[...], x.astype(BF16)).astype(out_ref.dtype)


def _dft1(m1, xs, cols):
    n2x2 = m1.shape[0]
    total = xs[0][0].shape[-1]
    ct = min(cols, total)
    in_specs = [pl.BlockSpec(m1.shape, lambda i: (0, 0))]
    args = [m1]
    for arr, b in xs:
        in_specs.append(pl.BlockSpec((1, arr.shape[1], ct), lambda i, b=b: (b, 0, i)))
        args.append(arr)
    return pl.pallas_call(
        _dft1_kernel, grid=(total // ct,), in_specs=in_specs,
        out_specs=pl.BlockSpec((n2x2, ct), lambda i: (0, i)),
        out_shape=jax.ShapeDtypeStruct((n2x2, total), BF16),
        compiler_params=_params("parallel"), name="dft_stage1")(*args)


def _twiddled(pg_ref, pgs_ref, txr_ref, txi_ref, j):
    return (pg_ref[...] * txr_ref[j:j + 1, :] + pgs_ref[...] * txi_ref[j:j + 1, :]).astype(BF16)


def _spectrum_kernel(a_ref, pg_ref, pgs_ref, txr_ref, txi_ref, l1_ref, kf_ref):
    kt = a_ref.shape[0]
    n1 = a_ref.shape[2]
    inv = 1.0 / l1_ref[...]
    for j in range(kt):
        gd = _twiddled(pg_ref, pgs_ref, txr_ref, txi_ref, j)
        x = _dot(gd, a_ref[j].reshape(2 * n1, a_ref.shape[3]))
        kf_ref[j] = (x * inv).reshape(2, n1, x.shape[1])


def _conv_mid_kernel(a_ref, kf_ref, pg_ref, pgs_ref, txr_ref, txi_ref, qd_ref, out_ref):
    kt = a_ref.shape[0]
    n1 = a_ref.shape[2]
    qd = qd_ref[...]
    for j in range(kt):
        gd = _twiddled(pg_ref, pgs_ref, txr_ref, txi_ref, j)
        x = _dot(gd, a_ref[j].reshape(2 * n1, a_ref.shape[3]))
        xr, xi = x[:n1], x[n1:]
        kr, ki = kf_ref[j, 0], kf_ref[j, 1]
        y = jnp.concatenate([xr * kr - xi * ki, xr * ki + xi * kr], axis=0).astype(BF16)
        out_ref[j] = _dot(qd, y).astype(out_ref.dtype).reshape(2, n1, x.shape[1])


def _dft_mid(a, tabs, kt, l1=None, kf=None):
    n2, _, n1, C = a.shape
    kt = min(kt, n2)
    blk = pl.BlockSpec((kt, 2, n1, C), lambda i: (i, 0, 0, 0))
    mat = pl.BlockSpec((2 * n1, 2 * n1), lambda i: (0, 0))
    tw = pl.BlockSpec((kt, 2 * n1), lambda i: (i, 0))
    if kf is None:
        return pl.pallas_call(
            _spectrum_kernel, grid=(n2 // kt,),
            in_specs=[blk, mat, mat, tw, tw, pl.BlockSpec((1, C), lambda i: (0, 0))],
            out_specs=blk, out_shape=jax.ShapeDtypeStruct(a.shape, F32),
            compiler_params=_params("parallel"), name="filter_spectrum")(
                a, tabs["pg"], tabs["pgs"], tabs["txr"], tabs["txi"], l1)
    return pl.pallas_call(
        _conv_mid_kernel, grid=(n2 // kt,),
        in_specs=[blk, blk, mat, mat, tw, tw, mat],
        out_specs=blk, out_shape=jax.ShapeDtypeStruct(a.shape, BF16),
        compiler_params=_params("parallel"), name="conv_mid")(
            a, kf, tabs["pg"], tabs["pgs"], tabs["txr"], tabs["txi"], tabs["qd"])


def _conv_out_kernel(b_ref, pp_ref, pps_ref, uxr_ref, uxi_ref, zc_ref, x0_ref, skip_ref, out_ref):
    nt = uxr_ref.shape[0]
    C = skip_ref.shape[1]
    h = zc_ref.shape[1]
    skip = skip_ref[...]
    for j in range(nt):
        cs = slice(j * C, (j + 1) * C)
        hm = (pp_ref[...] * uxr_ref[j:j + 1, :] + pps_ref[...] * uxi_ref[j:j + 1, :]).astype(BF16)
        y = _dot(hm, b_ref[:, cs])
        for b in range(2):
            yb = y[b * h:(b + 1) * h] + skip * zc_ref[b, :, cs]
            out_ref[b, :, cs] = (x0_ref[b, :, cs] * yb).astype(out_ref.dtype)


def _conv_out(bm, tabs, zc, x0, skip, nt):
    rows, total = bm.shape
    _, h, _ = zc.shape
    C = skip.shape[1]
    n1 = total // C
    nt = min(nt, n1)
    ct = nt * C
    dat = pl.BlockSpec((2, h, ct), lambda i: (0, 0, i))
    mat = pl.BlockSpec((2 * h, rows), lambda i: (0, 0))
    tw = pl.BlockSpec((nt, rows), lambda i: (i, 0))
    return pl.pallas_call(
        _conv_out_kernel, grid=(n1 // nt,),
        in_specs=[pl.BlockSpec((rows, ct), lambda i: (0, i)), mat, mat, tw, tw, dat, dat,
                  pl.BlockSpec((1, C), lambda i: (0, 0))],
        out_specs=dat, out_shape=jax.ShapeDtypeStruct(zc.shape, BF16),
        compiler_params=_params("parallel"), name="conv_out")(
            bm, tabs["pp"], tabs["pps"], tabs["uxr"], tabs["uxi"], zc, x0, skip)


def _hyena(zc, x0, skip, kern, l1):
    B, L, C = zc.shape
    assert B == 2, "the two batch rows are packed as one complex sequence"
    n1 = DFT_N1
    n2 = 2 * L // n1
    tabs = {k: jnp.asarray(v) for k, v in _dft_tables(n1, n2).items()}
    m1f = tabs["m1f"].astype(BF16)
    m1d = tabs["m1d"].astype(BF16)
    cols = 2048
    af = _dft1(m1f, [(kern.reshape(1, n2, n1 * C), 0)], cols)
    kf = _dft_mid(af.reshape(n2, 2, n1, C), tabs, 8, l1=l1)
    zc2 = zc.reshape(B, n2 // 2, n1 * C)
    a = _dft1(m1d, [(zc2, 0), (zc2, 1)], cols)
    bm = _dft_mid(a.reshape(n2, 2, n1, C), tabs, 8, kf=kf)
    out = _conv_out(bm.reshape(2 * n2, n1 * C), tabs, zc2, x0.reshape(B, n2 // 2, n1 * C), skip, 8)
    return out.reshape(B, L, C)


def _merge_kernel(x_ref, a_ref, b_ref, g_ref, wga_ref, bga_ref, wgb_ref, bgb_ref,
                  wa_ref, wb_ref, wo_ref, out_ref):
    x = x_ref[0]
    n = _rms(x, g_ref[...]).astype(BF16)
    ga = jax.nn.sigmoid(_dot(n, wga_ref[...]) + bga_ref[...])
    gb = jax.nn.sigmoid(_dot(n, wgb_ref[...]) + bgb_ref[...])
    merged = ga * _dot(a_ref[0], wa_ref[...]) + gb * _dot(b_ref[0], wb_ref[...])
    out_ref[0] = x + _dot(merged.astype(BF16), wo_ref[...])


def _merge(x, a, b, g, wga, bga, wgb, bgb, wa, wb, wo):
    B, L, D = x.shape
    tm = min(TOKEN_BLOCK, L)
    full = lambda w: pl.BlockSpec(w.shape, lambda bb, i: (0,) * w.ndim)
    tok = lambda w: pl.BlockSpec((1, tm, w), lambda bb, i: (bb, i, 0))
    ws = (g, wga, bga, wgb, bgb, wa, wb, wo)
    return pl.pallas_call(
        _merge_kernel, grid=(B, L // tm),
        in_specs=[tok(D), tok(a.shape[2]), tok(b.shape[2])] + [full(w) for w in ws],
        out_specs=tok(D), out_shape=jax.ShapeDtypeStruct(x.shape, F32),
        compiler_params=_params("parallel", "parallel"), name="merge")(x, a, b, *ws)


def _mlp_kernel(final, nchunk, x_ref, g_ref, w1_ref, w2_ref, gf_ref, out_ref):
    x = x_ref[0]
    n = _rms(x, g_ref[...]).astype(BF16)
    fc = w1_ref.shape[1] // nchunk
    acc = x
    for c in range(nchunk):
        hcl = jnp.maximum(_dot(n, w1_ref[:, c * fc:(c + 1) * fc]), 0.0)
        acc = acc + _dot((hcl * hcl).astype(BF16), w2_ref[c * fc:(c + 1) * fc, :])
    out_ref[0] = _rms(acc, gf_ref[...]) if final else acc


def _mlp(x, g, w1, w2, gf, final):
    B, L, D = x.shape
    tm = min(TOKEN_BLOCK, L)
    full = lambda w: pl.BlockSpec(w.shape, lambda bb, i: (0,) * w.ndim)
    tok = pl.BlockSpec((1, tm, D), lambda bb, i: (bb, i, 0))
    return pl.pallas_call(
        functools.partial(_mlp_kernel, final, 4), grid=(B, L // tm),
        in_specs=[tok, full(g), full(w1), full(w2), full(gf)],
        out_specs=tok, out_shape=jax.ShapeDtypeStruct(x.shape, F32),
        compiler_params=_params("parallel", "parallel"), name="mlp")(x, g, w1, w2, gf)


def _trunk(x, p):
    depth = p["w_in"].shape[0]
    D = x.shape[2]
    L = x.shape[1]
    dm = HEAD_DIM * MLSTM_HEADS
    ng = 4 * MLSTM_HEADS
    dh = p["hy_skip"].shape[1]
    o_g, o_u, o_ga = 4 * dm, 4 * dm + ng, 4 * dm + ng + 3 * dh
    fgrid = jnp.linspace(1e-4, HYENA_BANDS - 1, HYENA_BANDS, dtype=F32)
    fvec = jnp.zeros((1, FEAT_PAD), F32).at[0, 1:1 + HYENA_BANDS].set(fgrid)
    fvec = fvec.at[0, 1 + HYENA_BANDS:1 + 2 * HYENA_BANDS].set(fgrid)
    row = lambda a: a.reshape(1, -1)
    for l in range(depth):
        w_in = p["w_in"][l]
        b_in = p["b_in"][l]
        q, k, v, o, gates, x0, zc = _inproj(
            x, row(p["norm_mix"][l]),
            w_in[:, :o_g].astype(BF16), row(b_in[:o_g]),
            w_in[:, o_g:o_u].astype(BF16), row(b_in[o_g:o_u]),
            w_in[:, o_u:o_ga].astype(BF16), row(b_in[o_u:o_ga]),
            p["hy_conv_w"][l], row(p["hy_conv_b"][l]))
        g_row = jnp.swapaxes(gates, 1, 2)
        h_fwd = _mlstm_pass(q, k, v, gates, g_row, False)
        a_out = _mlstm_pass(q, k, v, gates, g_row, True, h_fwd, o, row(p["mlstm_norm"][l]))

        w1p = jnp.zeros((FEAT_PAD, p["hy_w1"].shape[2]), F32).at[:p["hy_w1"].shape[1]].set(p["hy_w1"][l])
        kern, l1 = _filter_taps(L, fvec, w1p, row(p["hy_b1"][l]), p["hy_w2"][l], row(p["hy_b2"][l]),
                                p["hy_w3"][l], row(p["hy_b3"][l]), row(p["hy_freq"][l]),
                                p["hy_w_fo"][l], p["hy_decay"][l])
        b_out = _hyena(zc, x0, row(p["hy_skip"][l]), kern, l1)

        x = _merge(x, a_out, b_out, row(p["norm_mix"][l]),
                   w_in[:, o_ga:o_ga + D].astype(BF16), row(b_in[o_ga:o_ga + D]),
                   w_in[:, o_ga + D:].astype(BF16), row(b_in[o_ga + D:]),
                   p["w_branch_a"][l].astype(BF16), p["w_branch_b"][l].astype(BF16),
                   p["w_out"][l].astype(BF16))
        x = _mlp(x, row(p["norm_mlp"][l]), p["w_mlp1"][l].astype(BF16), p["w_mlp2"][l].astype(BF16),
                 row(p["norm_final"]), l == depth - 1)
    return x


def kernel(x_prompt, x_sample, norm_mix, w_in, b_in, mlstm_norm, hy_conv_w, hy_conv_b, hy_w1, hy_b1, hy_w2, hy_b2, hy_w3, hy_b3, hy_freq, hy_w_fo, hy_decay, hy_skip, w_branch_a, w_branch_b, w_out, norm_mlp, w_mlp1, w_mlp2, norm_final):
    p = dict(norm_mix=norm_mix, w_in=w_in, b_in=b_in, mlstm_norm=mlstm_norm, hy_conv_w=hy_conv_w,
             hy_conv_b=hy_conv_b, hy_w1=hy_w1, hy_b1=hy_b1, hy_w2=hy_w2, hy_b2=hy_b2, hy_w3=hy_w3,
             hy_b3=hy_b3, hy_freq=hy_freq, hy_w_fo=hy_w_fo, hy_decay=hy_decay, hy_skip=hy_skip,
             w_branch_a=w_branch_a, w_branch_b=w_branch_b, w_out=w_out, norm_mlp=norm_mlp,
             w_mlp1=w_mlp1, w_mlp2=w_mlp2, norm_final=norm_final)
    return (_trunk(x_prompt, p), _trunk(x_sample, p))
```

```python
import functools
import math

import numpy as np
import jax
import jax.numpy as jnp
from jax import lax
from jax.experimental import pallas as pl
from jax.experimental.pallas import tpu as pltpu

F32 = jnp.float32
BF16 = jnp.bfloat16

EPS = 1e-6
MLSTM_HEADS = 4
HEAD_DIM = 128
CHUNK = 128
HYENA_BANDS = 16
FILTER_HIDDEN = 64
DFT_N1 = 128
SUBLANES = 8
TOKEN_BLOCK = 512
VMEM_LIMIT = 48 * 1024 * 1024
HIGHEST = lax.Precision.HIGHEST


def _params(*sem):
    return pltpu.CompilerParams(dimension_semantics=sem, vmem_limit_bytes=VMEM_LIMIT)


def _rms(x, g):
    return x * lax.rsqrt(jnp.mean(x * x, axis=-1, keepdims=True) + EPS) * g


def _dot(a, b):
    return jnp.dot(a, b, preferred_element_type=F32)


def _dot_nt(a, b):
    return lax.dot_general(a, b, (((1,), (1,)), ((), ())), preferred_element_type=F32)


def _split(a):
    hi = a.astype(BF16)
    return hi, (a - hi.astype(F32)).astype(BF16)


def _dot3(a, b_hi, b_lo):
    a_hi, a_lo = _split(a)
    return _dot(a_hi, b_hi) + (_dot(a_hi, b_lo) + _dot(a_lo, b_hi))


def _inproj_kernel(x_ref, xp_ref, xn_ref, g_ref, wm_ref, bm_ref, wkt_ref, bk_ref, wg_ref, bg_ref,
                   wgt_ref, bgt_ref, wu_ref, bu_ref, cw_ref, cb_ref,
                   q_ref, kt_ref, v_ref, o_ref, gc_ref, gr_ref, x0_ref, zc_ref):
    i = pl.program_id(1)
    last = pl.num_programs(1) - 1
    g = g_ref[...]
    n = _rms(x_ref[0], g).astype(BF16)
    dm = HEAD_DIM * MLSTM_HEADS
    pm = _dot(n, wm_ref[...]) + bm_ref[...]
    q_ref[0] = (pm[:, :dm] * (HEAD_DIM ** -0.5)).astype(BF16)
    v_ref[0] = pm[:, dm:2 * dm].astype(BF16)
    o_ref[0] = pm[:, 2 * dm:]
    kt_ref[0] = (_dot_nt(wkt_ref[...], n) + bk_ref[...]).astype(BF16)
    gc_ref[0] = _dot(n, wg_ref[...]) + bg_ref[...]
    gr_ref[0] = _dot_nt(wgt_ref[...], n) + bgt_ref[...]

    u = _dot(n, wu_ref[...]) + bu_ref[...]
    halo = jnp.concatenate([xp_ref[0], xn_ref[0]], axis=0)
    uh = _dot(_rms(halo, g).astype(BF16), wu_ref[...]) + bu_ref[...]
    u_prev = jnp.where(i > 0, uh[SUBLANES - 1:SUBLANES], 0.0)
    u_next = jnp.where(i < last, uh[SUBLANES:SUBLANES + 1], 0.0)
    tm = u.shape[0]
    row = lax.broadcasted_iota(jnp.int32, (tm, 1), 0)
    u_m1 = jnp.where(row == 0, u_prev, pltpu.roll(u, 1, 0))
    u_p1 = jnp.where(row == tm - 1, u_next, pltpu.roll(u, tm - 1, 0))
    cw = cw_ref[...]
    c = u_m1 * cw[0:1] + u * cw[1:2] + u_p1 * cw[2:3] + cb_ref[...]
    dh = c.shape[1] // 3
    x0_ref[0] = c[:, :dh]
    zc_ref[0] = c[:, dh:2 * dh] * c[:, 2 * dh:]


def _inproj(x, g, wm, bm, wkt, bk, wg, bg, wgt, bgt, wu, bu, cw, cb):
    B, L, D = x.shape
    tm = min(TOKEN_BLOCK, L)
    nb = L // tm
    r8 = tm // SUBLANES
    dm = wkt.shape[0]
    dh = wu.shape[1] // 3
    ng = wg.shape[1]
    ws = (g, wm, bm, wkt, bk, wg, bg, wgt, bgt, wu, bu, cw, cb)
    full = lambda a: pl.BlockSpec(a.shape, lambda b, i: (0,) * a.ndim)
    tok = lambda w: pl.BlockSpec((1, tm, w), lambda b, i: (b, i, 0))
    tr = lambda w: pl.BlockSpec((1, w, tm), lambda b, i: (b, 0, i))
    in_specs = [
        tok(D),
        pl.BlockSpec((1, SUBLANES, D), lambda b, i: (b, jnp.maximum(i * r8 - 1, 0), 0)),
        pl.BlockSpec((1, SUBLANES, D), lambda b, i: (b, jnp.minimum((i + 1) * r8, L // SUBLANES - 1), 0)),
    ] + [full(w) for w in ws]
    sds = jax.ShapeDtypeStruct
    out_shape = (sds((B, L, dm), BF16), sds((B, dm, L), BF16), sds((B, L, dm), BF16), sds((B, L, dm), F32),
                 sds((B, L, ng), F32), sds((B, ng, L), F32), sds((B, L, dh), F32), sds((B, L, dh), F32))
    out_specs = (tok(dm), tr(dm), tok(dm), tok(dm), tok(ng), tr(ng), tok(dh), tok(dh))
    return pl.pallas_call(
        _inproj_kernel, grid=(B, nb), in_specs=in_specs, out_specs=out_specs,
        out_shape=out_shape, compiler_params=_params("parallel", "parallel"),
        name="inproj")(x, x, x, *ws)


def _log_sigmoid(x):
    return jnp.minimum(x, 0.0) - jnp.log1p(jnp.exp(-jnp.abs(x)))


def _mlstm_kernel(reverse, finish, *refs):
    if finish:
        q_ref, kt_ref, v_ref, gc_ref, gr_ref, hf_ref, o_ref, hn_ref, out_ref, c_s, m_s = refs
    else:
        q_ref, kt_ref, v_ref, gc_ref, gr_ref, out_ref, c_s, m_s = refs

    @pl.when(pl.program_id(0) == 0)
    def _():
        c_s[...] = jnp.zeros_like(c_s)
        m_s[...] = jnp.zeros_like(m_s)

    nh = MLSTM_HEADS
    nb = q_ref.shape[0]
    nq = nb * nh
    ch = q_ref.shape[1]
    off = 2 * nh if reverse else 0
    ri = lax.broadcasted_iota(jnp.int32, (ch, ch), 0)
    ci = lax.broadcasted_iota(jnp.int32, (ch, ch), 1)
    mask = (ri <= ci) if reverse else (ci <= ri)
    tri_col = mask.astype(F32)
    tri_row = ((ci <= ri) if reverse else (ri <= ci)).astype(F32)

    li_col = jnp.concatenate([gc_ref[b][:, off:off + nh] for b in range(nb)], axis=1)
    lf_col = _log_sigmoid(jnp.concatenate([gc_ref[b][:, off + nh:off + 2 * nh] for b in range(nb)], axis=1))
    li_row = jnp.concatenate([gr_ref[b][off:off + nh, :] for b in range(nb)], axis=0)
    lf_row = _log_sigmoid(jnp.concatenate([gr_ref[b][off + nh:off + 2 * nh, :] for b in range(nb)], axis=0))
    b_col = jnp.dot(tri_col, lf_col, precision=HIGHEST, preferred_element_type=F32)
    b_row = jnp.dot(lf_row, tri_row, precision=HIGHEST, preferred_element_type=F32)
    beta_col = li_col - b_col
    beta_row = li_row - b_row
    cmx = beta_col
    rowi = lax.broadcasted_iota(jnp.int32, (ch, nq), 0)
    k = 1
    while k < ch:
        if reverse:
            cmx = jnp.maximum(cmx, jnp.where(rowi < ch - k, pltpu.roll(cmx, ch - k, 0), -jnp.inf))
        else:
            cmx = jnp.maximum(cmx, jnp.where(rowi >= k, pltpu.roll(cmx, k, 0), -jnp.inf))
        k *= 2
    last = 0 if reverse else ch - 1
    g_row = b_col[last:last + 1, :]
    m_loc = g_row + cmx[last:last + 1, :]
    m_prev = m_s[0:1, 0:nq]
    mx = jnp.maximum(m_prev, cmx)
    s_inter = jnp.exp(m_prev - mx)
    inv_floor = jnp.exp(-mx - b_col)
    m_new = jnp.maximum(g_row + m_prev, m_loc)
    s_old = jnp.exp(g_row + m_prev - m_new)
    s_new = jnp.exp(m_loc - m_new)
    w_row = jnp.exp(beta_row - jnp.max(beta_row, axis=1, keepdims=True))
    m_s[0:1, 0:nq] = m_new

    ones_col = (lax.broadcasted_iota(jnp.int32, (ch, HEAD_DIM), 1) == 0).astype(BF16)
    for b in range(nb):
        for h in range(nh):
            i = b * nh + h
            hs = slice(h * HEAD_DIM, (h + 1) * HEAD_DIM)
            q = q_ref[b, :, hs]
            kt = kt_ref[b, hs, :]
            v_aug = jnp.concatenate([v_ref[b, :, hs], ones_col], axis=1)
            d_mat = jnp.exp(jnp.where(mask, beta_row[i:i + 1, :] - mx[:, i:i + 1], -jnp.inf))
            scores = (_dot(q, kt) * d_mat).astype(BF16)
            c_prev = c_s[i]
            num = _dot(scores, v_aug) + s_inter[:, i:i + 1] * _dot(q, c_prev.astype(BF16))
            den = num[:, HEAD_DIM:HEAD_DIM + 1]
            hh = num[:, :HEAD_DIM] * (1.0 / jnp.maximum(jnp.abs(den), inv_floor[:, i:i + 1]))
            ktw = (kt.astype(F32) * w_row[i:i + 1, :]).astype(BF16)
            c_s[i] = s_old[:, i:i + 1] * c_prev + s_new[:, i:i + 1] * _dot(ktw, v_aug)
            if finish:
                hs_sum = hh + hf_ref[b, :, hs]
                mu = jnp.mean(hs_sum, axis=1, keepdims=True)
                dv = hs_sum - mu
                var = jnp.mean(dv * dv, axis=1, keepdims=True)
                y = dv * lax.rsqrt(var + EPS) * hn_ref[:, hs] * jax.nn.sigmoid(o_ref[b, :, hs])
                out_ref[b, :, hs] = y.astype(out_ref.dtype)
            else:
                out_ref[b, :, hs] = hh


def _mlstm_pass(q, kt, v, g_col, g_row, reverse, h_fwd=None, o=None, head_norm=None):
    B, L, dm = q.shape
    ch = CHUNK
    nc = L // ch
    finish = h_fwd is not None
    cidx = (lambda c: nc - 1 - c) if reverse else (lambda c: c)
    tok = lambda w: pl.BlockSpec((B, ch, w), lambda c: (0, cidx(c), 0))
    tr = lambda w: pl.BlockSpec((B, w, ch), lambda c: (0, 0, cidx(c)))
    in_specs = [tok(dm), tr(dm), tok(dm), tok(g_col.shape[2]), tr(g_row.shape[1])]
    args = [q, kt, v, g_col, g_row]
    if finish:
        in_specs += [tok(dm), tok(dm), pl.BlockSpec(head_norm.shape, lambda c: (0, 0))]
        args += [h_fwd, o, head_norm]
    return pl.pallas_call(
        functools.partial(_mlstm_kernel, reverse, finish),
        grid=(nc,), in_specs=in_specs, out_specs=tok(dm),
        out_shape=jax.ShapeDtypeStruct((B, L, dm), BF16 if finish else F32),
        scratch_shapes=[pltpu.VMEM((B * MLSTM_HEADS, HEAD_DIM, 2 * HEAD_DIM), F32),
                        pltpu.VMEM((SUBLANES, 128), F32)],
        compiler_params=_params("arbitrary"),
        name="mlstm_bwd" if reverse else "mlstm_fwd")(*args)


def _filter_kernel(seq_len, fvec_ref, ph_ref, w1h_ref, w1l_ref, b1_ref, w2h_ref, w2l_ref, b2_ref,
                   w3h_ref, w3l_ref, b3_ref, fr_ref, wfh_ref, wfl_ref, dec_ref, kern_ref, l1_ref):
    i = pl.program_id(0)
    half = kern_ref.shape[0] // 2
    hid = FILTER_HIDDEN
    L = seq_len

    def lag(n):
        return jnp.where(n < L, n, jnp.where(n == L, 0, 2 * L - n)).astype(F32)

    base = i * 2 * half
    lane = lax.broadcasted_iota(jnp.int32, (half, 2 * hid), 1)
    grp = lane >= hid
    feat = lane & (hid - 1)
    n = base + lax.broadcasted_iota(jnp.int32, (half, 2 * hid), 0) + jnp.where(grp, half, 0)
    j = lag(n)
    t = j / (L - 1)
    w = ((2.0 * math.pi) * j) / L
    z = jnp.where(feat == 0, t,
                  jnp.where(feat <= 2 * HYENA_BANDS, jnp.cos(fvec_ref[...] * w + ph_ref[...]), 0.0))
    fr = fr_ref[...]
    h = jnp.sin(fr * (_dot3(z, w1h_ref[...], w1l_ref[...]) + b1_ref[...]))
    h = jnp.sin(fr * (_dot3(h, w2h_ref[...], w2l_ref[...]) + b2_ref[...]))
    h = jnp.sin(fr * (_dot3(h, w3h_ref[...], w3l_ref[...]) + b3_ref[...]))

    @pl.when(i == 0)
    def _():
        l1_ref[...] = jnp.zeros_like(l1_ref)

    dec = jnp.abs(dec_ref[0])
    for gi in range(2):
        hg = jnp.where(grp == (gi == 1), h, 0.0)
        n_col = base + gi * half + lax.broadcasted_iota(jnp.int32, (half, 1), 0)
        taps = _dot3(hg, wfh_ref[...], wfl_ref[...]) * jnp.exp(-(lag(n_col) / (L - 1)) * dec)
        l1_ref[...] += jnp.sum(jnp.abs(taps), axis=0, keepdims=True)
        kern_ref[gi * half:(gi + 1) * half, :] = jnp.where(n_col == L, 0.0, taps)


def _blockdiag2(w):
    z = jnp.zeros_like(w)
    return jnp.concatenate([jnp.concatenate([w, z], axis=1), jnp.concatenate([z, w], axis=1)], axis=0)


def _filter_taps(L, w1, b1, w2, b2, w3, b3, fr, wfo, dec):
    C = wfo.shape[1] // 2
    hid = FILTER_HIDDEN
    rows = min(512, L)
    nblk = 2 * L // rows
    fgrid = jnp.linspace(1e-4, HYENA_BANDS - 1, HYENA_BANDS, dtype=F32)
    fhalf = jnp.zeros((hid,), F32).at[1:1 + HYENA_BANDS].set(fgrid).at[1 + HYENA_BANDS:1 + 2 * HYENA_BANDS].set(fgrid)
    phalf = jnp.zeros((hid,), F32).at[1 + HYENA_BANDS:1 + 2 * HYENA_BANDS].set(0.5 * math.pi)
    two = lambda a: jnp.concatenate([a.reshape(1, -1), a.reshape(1, -1)], axis=1)
    w1p = jnp.zeros((hid, hid), F32).at[:w1.shape[0]].set(w1)
    mats = []
    for wmat in (_blockdiag2(w1p), _blockdiag2(w2), _blockdiag2(w3)):
        mats.append(_split(wmat))
    wst = jnp.concatenate([wfo, wfo], axis=0)
    wfh, wfl = _split(wst)
    args = (two(fhalf), two(phalf), mats[0][0], mats[0][1], two(b1), mats[1][0], mats[1][1], two(b2),
            mats[2][0], mats[2][1], two(b3), two(fr))
    full = lambda a: pl.BlockSpec(a.shape, lambda i: (0,) * a.ndim)
    dirn = lambda i: (i * rows) // L
    in_specs = [full(a) for a in args] + [
        pl.BlockSpec((2 * hid, C), lambda i: (0, dirn(i))), pl.BlockSpec((2 * hid, C), lambda i: (0, dirn(i))),
        pl.BlockSpec((1, 1, C), lambda i: (dirn(i), 0, 0))]
    return pl.pallas_call(
        functools.partial(_filter_kernel, L),
        grid=(nblk,), in_specs=in_specs,
        out_specs=(pl.BlockSpec((rows, C), lambda i: (i, 0)), pl.BlockSpec((1, C), lambda i: (0, 0))),
        out_shape=(jax.ShapeDtypeStruct((2 * L, C), F32), jax.ShapeDtypeStruct((1, C), F32)),
        compiler_params=_params("arbitrary"),
        name="hyena_filter")(*args, wfh, wfl, dec.reshape(2, 1, C))


@functools.lru_cache(maxsize=None)
def _dft_tables(n1, n2):
    n = n1 * n2
    h = n2 // 2
    a2 = 2.0 * np.pi * np.outer(np.arange(n2), np.arange(n2)) / n2
    c2, s2 = np.cos(a2), np.sin(a2)
    m1d = np.empty((n2, 2, n2))
    m1d[:, 0, :h], m1d[:, 0, h:] = c2[:, :h], s2[:, :h]
    m1d[:, 1, :h], m1d[:, 1, h:] = -s2[:, :h], c2[:, :h]
    m1f = np.stack([c2, -s2], axis=1)
    a1 = 2.0 * np.pi * np.outer(np.arange(n1), np.arange(n1)) / n1
    f1r, f1i = np.cos(a1), -np.sin(a1)
    pg = np.block([[f1r, -f1i], [f1i, f1r]])
    pgs = np.block([[-f1i, -f1r], [f1r, -f1i]])
    qd = np.block([[f1r, f1i], [-f1i, f1r]])
    at = 2.0 * np.pi * np.outer(np.arange(n2), np.arange(n1)) / n
    tr, ti = np.cos(at), -np.sin(at)
    txr = np.concatenate([tr, tr], axis=1)
    txi = np.concatenate([ti, ti], axis=1)
    er, ei = c2[:h] / n, s2[:h] / n
    pp = np.block([[er, -ei], [ei, er]])
    pps = np.block([[-ei, -er], [er, -ei]])
    ur, ui = np.cos(at).T, np.sin(at).T
    uxr = np.concatenate([ur, ur], axis=1)
    uxi = np.concatenate([ui, ui], axis=1)
    f32 = lambda a: np.asarray(a, np.float32)
    return dict(m1d=f32(m1d.reshape(2 * n2, n2)), m1f=f32(m1f.reshape(2 * n2, n2)),
                pg=f32(pg), pgs=f32(pgs), qd=f32(qd), txr=f32(txr), txi=f32(txi),
                pp=f32(pp), pps=f32(pps), uxr=f32(uxr), uxi=f32(uxi))


def _dft1_kernel(m_ref, x_ref, out_ref):
    nt = out_ref.shape[0]
    m = m_ref[...]
    for j in range(nt):
        parts = [x_ref[b, :, j, :] for b in range(x_ref.shape[0])]
        x = parts[0] if len(parts) == 1 else jnp.concatenate(parts, axis=0)
        out_ref[j] = _dot(m, x.astype(BF16))


def _dft1(m1, x):
    parts, rows, n1, C = x.shape
    nt = SUBLANES
    return pl.pallas_call(
        _dft1_kernel, grid=(n1 // nt,),
        in_specs=[pl.BlockSpec(m1.shape, lambda i: (0, 0)),
                  pl.BlockSpec((parts, rows, nt, C), lambda i: (0, 0, i, 0))],
        out_specs=pl.BlockSpec((nt, m1.shape[0], C), lambda i: (i, 0, 0)),
        out_shape=jax.ShapeDtypeStruct((n1, m1.shape[0], C), F32),
        compiler_params=_params("parallel"), name="dft_stage1")(m1, x)


def _stage2(a_ref, pg_ref, pgs_ref, txr_ref, txi_ref, j):
    gd = (pg_ref[...] * txr_ref[j:j + 1, :] + pgs_ref[...] * txi_ref[j:j + 1, :]).astype(BF16)
    x = jnp.concatenate([a_ref[:, 2 * j, :], a_ref[:, 2 * j + 1, :]], axis=0).astype(BF16)
    return _dot(gd, x)


def _spectrum_kernel(a_ref, pg_ref, pgs_ref, txr_ref, txi_ref, l1_ref, kf_ref):
    inv = 1.0 / l1_ref[...]
    for j in range(kf_ref.shape[0]):
        x = _stage2(a_ref, pg_ref, pgs_ref, txr_ref, txi_ref, j)
        kf_ref[j] = (x * inv).astype(kf_ref.dtype).reshape(kf_ref.shape[1:])


def _conv_mid_kernel(a_ref, kf_ref, pg_ref, pgs_ref, txr_ref, txi_ref, qd_ref, out_ref):
    n1 = a_ref.shape[0]
    qd = qd_ref[...]
    for j in range(out_ref.shape[0]):
        x = _stage2(a_ref, pg_ref, pgs_ref, txr_ref, txi_ref, j)
        xr, xi = x[:n1], x[n1:]
        kr, ki = kf_ref[j, 0].astype(F32), kf_ref[j, 1].astype(F32)
        y = jnp.concatenate([xr * kr - xi * ki, xr * ki + xi * kr], axis=0).astype(BF16)
        out_ref[j] = _dot(qd, y).reshape(out_ref.shape[1:])


def _dft_mid(a, tabs, l1=None, kf=None):
    n1, rows, C = a.shape
    n2 = rows // 2
    kt = SUBLANES
    ablk = pl.BlockSpec((n1, 2 * kt, C), lambda i: (0, i, 0))
    blk = pl.BlockSpec((kt, 2, n1, C), lambda i: (i, 0, 0, 0))
    mat = pl.BlockSpec((2 * n1, 2 * n1), lambda i: (0, 0))
    tw = pl.BlockSpec((kt, 2 * n1), lambda i: (i, 0))
    if kf is None:
        return pl.pallas_call(
            _spectrum_kernel, grid=(n2 // kt,),
            in_specs=[ablk, mat, mat, tw, tw, pl.BlockSpec((1, C), lambda i: (0, 0))],
            out_specs=blk, out_shape=jax.ShapeDtypeStruct((n2, 2, n1, C), BF16),
            compiler_params=_params("parallel"), name="filter_spectrum")(
                a, tabs["pg"], tabs["pgs"], tabs["txr"], tabs["txi"], l1)
    return pl.pallas_call(
        _conv_mid_kernel, grid=(n2 // kt,),
        in_specs=[ablk, blk, mat, mat, tw, tw, mat],
        out_specs=blk, out_shape=jax.ShapeDtypeStruct((n2, 2, n1, C), F32),
        compiler_params=_params("parallel"), name="conv_mid")(
            a, kf, tabs["pg"], tabs["pgs"], tabs["txr"], tabs["txi"], tabs["qd"])


def _conv_out_kernel(b_ref, pp_ref, pps_ref, uxr_ref, uxi_ref, out_ref):
    h = out_ref.shape[1]
    for j in range(out_ref.shape[2]):
        hm = (pp_ref[...] * uxr_ref[j:j + 1, :] + pps_ref[...] * uxi_ref[j:j + 1, :]).astype(BF16)
        bcat = jnp.concatenate([b_ref[:, 0, j, :], b_ref[:, 1, j, :]], axis=0).astype(BF16)
        y = _dot(hm, bcat)
        out_ref[0, :, j, :] = y[:h]
        out_ref[1, :, j, :] = y[h:]


def _conv_out(bm, tabs):
    n2, _, n1, C = bm.shape
    nt = SUBLANES
    mat = pl.BlockSpec((n2, 2 * n2), lambda i: (0, 0))
    tw = pl.BlockSpec((nt, 2 * n2), lambda i: (i, 0))
    return pl.pallas_call(
        _conv_out_kernel, grid=(n1 // nt,),
        in_specs=[pl.BlockSpec((n2, 2, nt, C), lambda i: (0, 0, i, 0)), mat, mat, tw, tw],
        out_specs=pl.BlockSpec((2, n2 // 2, nt, C), lambda i: (0, 0, i, 0)),
        out_shape=jax.ShapeDtypeStruct((2, n2 // 2, n1, C), F32),
        compiler_params=_params("parallel"), name="conv_out")(
            bm, tabs["pp"], tabs["pps"], tabs["uxr"], tabs["uxi"])


def _long_conv(zc, kern, l1):
    B, L, C = zc.shape
    assert B == 2, "the two batch rows are packed as one complex sequence"
    n1 = DFT_N1
    n2 = 2 * L // n1
    tabs = {k: jnp.asarray(v) for k, v in _dft_tables(n1, n2).items()}
    af = _dft1(tabs["m1f"].astype(BF16), kern.reshape(2, n2 // 2, n1, C))
    kf = _dft_mid(af, tabs, l1=l1)
    a = _dft1(tabs["m1d"].astype(BF16), zc.reshape(B, n2 // 2, n1, C))
    bm = _dft_mid(a, tabs, kf=kf)
    return _conv_out(bm, tabs).reshape(B, L, C)


def _merge_kernel(x_ref, a_ref, y_ref, zc_ref, x0_ref, skip_ref, g_ref, wga_ref, bga_ref, wgb_ref,
                  bgb_ref, wa_ref, wb_ref, wo_ref, out_ref):
    x = x_ref[0]
    n = _rms(x, g_ref[...]).astype(BF16)
    ga = jax.nn.sigmoid(_dot(n, wga_ref[...]) + bga_ref[...])
    gb = jax.nn.sigmoid(_dot(n, wgb_ref[...]) + bgb_ref[...])
    hy = (x0_ref[0] * (y_ref[0] + skip_ref[...] * zc_ref[0])).astype(BF16)
    merged = ga * _dot(a_ref[0], wa_ref[...]) + gb * _dot(hy, wb_ref[...])
    out_ref[0] = x + _dot(merged.astype(BF16), wo_ref[...])


def _merge(x, a, y, zc, x0, skip, g, wga, bga, wgb, bgb, wa, wb, wo):
    B, L, D = x.shape
    tm = min(TOKEN_BLOCK, L)
    full = lambda w: pl.BlockSpec(w.shape, lambda bb, i: (0,) * w.ndim)
    tok = lambda w: pl.BlockSpec((1, tm, w), lambda bb, i: (bb, i, 0))
    ws = (skip, g, wga, bga, wgb, bgb, wa, wb, wo)
    dh = y.shape[2]
    return pl.pallas_call(
        _merge_kernel, grid=(B, L // tm),
        in_specs=[tok(D), tok(a.shape[2]), tok(dh), tok(dh), tok(dh)] + [full(w) for w in ws],
        out_specs=tok(D), out_shape=jax.ShapeDtypeStruct(x.shape, F32),
        compiler_params=_params("parallel", "parallel"), name="merge")(x, a, y, zc, x0, *ws)


def _mlp_kernel(final, nchunk, x_ref, g_ref, w1_ref, w2_ref, gf_ref, out_ref):
    x = x_ref[0]
    n = _rms(x, g_ref[...]).astype(BF16)
    fc = w1_ref.shape[1] // nchunk
    acc = x
    for c in range(nchunk):
        hcl = jnp.maximum(_dot(n, w1_ref[:, c * fc:(c + 1) * fc]), 0.0)
        acc = acc + _dot((hcl * hcl).astype(BF16), w2_ref[c * fc:(c + 1) * fc, :])
    out_ref[0] = _rms(acc, gf_ref[...]) if final else acc


def _mlp(x, g, w1, w2, gf, final):
    B, L, D = x.shape
    tm = min(TOKEN_BLOCK, L)
    full = lambda w: pl.BlockSpec(w.shape, lambda bb, i: (0,) * w.ndim)
    tok = pl.BlockSpec((1, tm, D), lambda bb, i: (bb, i, 0))
    return pl.pallas_call(
        functools.partial(_mlp_kernel, final, 4), grid=(B, L // tm),
        in_specs=[tok, full(g), full(w1), full(w2), full(gf)],
        out_specs=tok, out_shape=jax.ShapeDtypeStruct(x.shape, F32),
        compiler_params=_params("parallel", "parallel"), name="mlp")(x, g, w1, w2, gf)


def _trunk(x, p):
    depth = p["w_in"].shape[0]
    D = x.shape[2]
    L = x.shape[1]
    dm = HEAD_DIM * MLSTM_HEADS
    ng = 4 * MLSTM_HEADS
    dh = p["hy_skip"].shape[1]
    o_g, o_u, o_ga = 4 * dm, 4 * dm + ng, 4 * dm + ng + 3 * dh
    row = lambda a: a.reshape(1, -1)
    col = lambda a: a.reshape(-1, 1)
    for l in range(depth):
        w_in = p["w_in"][l]
        b_in = p["b_in"][l]
        w_qvo = jnp.concatenate([w_in[:, :dm], w_in[:, 2 * dm:4 * dm]], axis=1)
        b_qvo = jnp.concatenate([b_in[:dm], b_in[2 * dm:4 * dm]])
        w_g = w_in[:, o_g:o_u]
        q, kt, v, o, g_col, g_row, x0, zc = _inproj(
            x, row(p["norm_mix"][l]), w_qvo.astype(BF16), row(b_qvo),
            w_in[:, dm:2 * dm].T.astype(BF16), col(b_in[dm:2 * dm]),
            w_g.astype(BF16), row(b_in[o_g:o_u]), w_g.T.astype(BF16), col(b_in[o_g:o_u]),
            w_in[:, o_u:o_ga].astype(BF16), row(b_in[o_u:o_ga]),
            p["hy_conv_w"][l], row(p["hy_conv_b"][l]))
        h_fwd = _mlstm_pass(q, kt, v, g_col, g_row, False)
        a_out = _mlstm_pass(q, kt, v, g_col, g_row, True, h_fwd, o, row(p["mlstm_norm"][l]))

        kern, l1 = _filter_taps(L, p["hy_w1"][l], p["hy_b1"][l], p["hy_w2"][l], p["hy_b2"][l],
                                p["hy_w3"][l], p["hy_b3"][l], p["hy_freq"][l],
                                p["hy_w_fo"][l], p["hy_decay"][l])
        y = _long_conv(zc, kern, l1)

        x = _merge(x, a_out, y, zc, x0, row(p["hy_skip"][l]), row(p["norm_mix"][l]),
                   w_in[:, o_ga:o_ga + D].astype(BF16), row(b_in[o_ga:o_ga + D]),
                   w_in[:, o_ga + D:].astype(BF16), row(b_in[o_ga + D:]),
                   p["w_branch_a"][l].astype(BF16), p["w_branch_b"][l].astype(BF16),
                   p["w_out"][l].astype(BF16))
        x = _mlp(x, row(p["norm_mlp"][l]), p["w_mlp1"][l].astype(BF16), p["w_mlp2"][l].astype(BF16),
                 row(p["norm_final"]), l == depth - 1)
    return x


def kernel(x_prompt, x_sample, norm_mix, w_in, b_in, mlstm_norm, hy_conv_w, hy_conv_b, hy_w1, hy_b1, hy_w2, hy_b2, hy_w3, hy_b3, hy_freq, hy_w_fo, hy_decay, hy_skip, w_branch_a, w_branch_b, w_out, norm_mlp, w_mlp1, w_mlp2, norm_final):
    p = dict(norm_mix=norm_mix, w_in=w_in, b_in=b_in, mlstm_norm=mlstm_norm, hy_conv_w=hy_conv_w,
             hy_conv_b=hy_conv_b, hy_w1=hy_w1, hy_b1=hy_b1, hy_w2=hy_w2, hy_b2=hy_b2, hy_w3=hy_w3,
             hy_b3=hy_b3, hy_freq=hy_freq, hy_w_fo=hy_w_fo, hy_decay=hy_decay, hy_skip=hy_skip,
             w_branch_a=w_branch_a, w_branch_b=w_branch_b, w_out=w_out, norm_mlp=norm_mlp,
             w_mlp1=w_mlp1, w_mlp2=w_mlp2, norm_final=norm_final)
    return (_trunk(x_prompt, p), _trunk(x_sample, p))
```

```python
import functools
import math

import numpy as np
import jax
import jax.numpy as jnp
from jax import lax
from jax.experimental import pallas as pl
from jax.experimental.pallas import tpu as pltpu

F32 = jnp.float32
BF16 = jnp.bfloat16

EPS = 1e-6
MLSTM_HEADS = 4
HEAD_DIM = 128
CHUNK = 256
HYENA_BANDS = 16
FILTER_HIDDEN = 64
DFT_N1 = 128
SUBLANES = 8
TOKEN_BLOCK = 512
VMEM_LIMIT = 48 * 1024 * 1024
HIGHEST = lax.Precision.HIGHEST


def _params(*sem):
    return pltpu.CompilerParams(dimension_semantics=sem, vmem_limit_bytes=VMEM_LIMIT)


def _rms(x, g):
    return x * lax.rsqrt(jnp.mean(x * x, axis=-1, keepdims=True) + EPS) * g


def _dot(a, b):
    return jnp.dot(a, b, preferred_element_type=F32)


def _dot_nt(a, b):
    return lax.dot_general(a, b, (((1,), (1,)), ((), ())), preferred_element_type=F32)


def _split(a):
    hi = a.astype(BF16)
    return hi, (a - hi.astype(F32)).astype(BF16)


def _dot3(a, b_hi, b_lo):
    a_hi, a_lo = _split(a)
    return _dot(a_hi, b_hi) + (_dot(a_hi, b_lo) + _dot(a_lo, b_hi))


def _inproj_kernel(x_ref, xp_ref, xn_ref, g_ref, wm_ref, bm_ref, wkt_ref, bk_ref, wg_ref, bg_ref,
                   wgt_ref, bgt_ref, wu_ref, bu_ref, cw_ref, cb_ref,
                   q_ref, kt_ref, v_ref, o_ref, gc_ref, gr_ref, x0_ref, zc_ref):
    i = pl.program_id(1)
    last = pl.num_programs(1) - 1
    g = g_ref[...]
    n = _rms(x_ref[0], g).astype(BF16)
    dm = HEAD_DIM * MLSTM_HEADS
    pm = _dot(n, wm_ref[...]) + bm_ref[...]
    q_ref[0] = (pm[:, :dm] * (HEAD_DIM ** -0.5)).astype(BF16)
    v_ref[0] = pm[:, dm:2 * dm].astype(BF16)
    o_ref[0] = pm[:, 2 * dm:]
    kt_ref[0] = (_dot_nt(wkt_ref[...], n) + bk_ref[...]).astype(BF16)
    gc_ref[0] = _dot(n, wg_ref[...]) + bg_ref[...]
    gr_ref[0] = _dot_nt(wgt_ref[...], n) + bgt_ref[...]

    tm = n.shape[0]
    halo = _rms(jnp.concatenate([xp_ref[0], xn_ref[0]], axis=0), g).astype(BF16)
    u_all = _dot(jnp.concatenate([n, halo], axis=0), wu_ref[...]) + bu_ref[...]
    u = u_all[:tm]
    u_prev = jnp.where(i > 0, u_all[tm + SUBLANES - 1:tm + SUBLANES], 0.0)
    u_next = jnp.where(i < last, u_all[tm + SUBLANES:tm + SUBLANES + 1], 0.0)
    row = lax.broadcasted_iota(jnp.int32, (tm, 1), 0)
    u_m1 = jnp.where(row == 0, u_prev, pltpu.roll(u, 1, 0))
    u_p1 = jnp.where(row == tm - 1, u_next, pltpu.roll(u, tm - 1, 0))
    cw = cw_ref[...]
    c = u_m1 * cw[0:1] + u * cw[1:2] + u_p1 * cw[2:3] + cb_ref[...]
    dh = c.shape[1] // 3
    x0_ref[0] = c[:, :dh]
    zc_ref[0] = c[:, dh:2 * dh] * c[:, 2 * dh:]


def _inproj(x, g, wm, bm, wkt, bk, wg, bg, wgt, bgt, wu, bu, cw, cb):
    B, L, D = x.shape
    tm = min(TOKEN_BLOCK, L)
    nb = L // tm
    r8 = tm // SUBLANES
    dm = wkt.shape[0]
    dh = wu.shape[1] // 3
    ng = wg.shape[1]
    ws = (g, wm, bm, wkt, bk, wg, bg, wgt, bgt, wu, bu, cw, cb)
    full = lambda a: pl.BlockSpec(a.shape, lambda b, i: (0,) * a.ndim)
    tok = lambda w: pl.BlockSpec((1, tm, w), lambda b, i: (b, i, 0))
    tr = lambda w: pl.BlockSpec((1, w, tm), lambda b, i: (b, 0, i))
    in_specs = [
        tok(D),
        pl.BlockSpec((1, SUBLANES, D), lambda b, i: (b, jnp.maximum(i * r8 - 1, 0), 0)),
        pl.BlockSpec((1, SUBLANES, D), lambda b, i: (b, jnp.minimum((i + 1) * r8, L // SUBLANES - 1), 0)),
    ] + [full(w) for w in ws]
    sds = jax.ShapeDtypeStruct
    out_shape = (sds((B, L, dm), BF16), sds((B, dm, L), BF16), sds((B, L, dm), BF16), sds((B, L, dm), F32),
                 sds((B, L, ng), F32), sds((B, ng, L), F32), sds((B, L, dh), F32), sds((B, L, dh), F32))
    out_specs = (tok(dm), tr(dm), tok(dm), tok(dm), tok(ng), tr(ng), tok(dh), tok(dh))
    return pl.pallas_call(
        _inproj_kernel, grid=(B, nb), in_specs=in_specs, out_specs=out_specs,
        out_shape=out_shape, compiler_params=_params("parallel", "parallel"),
        name="inproj")(x, x, x, *ws)


def _log_sigmoid(x):
    return jnp.minimum(x, 0.0) - jnp.log1p(jnp.exp(-jnp.abs(x)))


def _mlstm_dir(reverse, q_ref, kt_ref, v_ref, gc_ref, gr_ref, out_ref, c_s, m_s, slot):
    nh = MLSTM_HEADS
    nb = q_ref.shape[0]
    nq = nb * nh
    ch = q_ref.shape[1]
    off = 2 * nh if reverse else 0
    ri = lax.broadcasted_iota(jnp.int32, (ch, ch), 0)
    ci = lax.broadcasted_iota(jnp.int32, (ch, ch), 1)
    mask = (ri <= ci) if reverse else (ci <= ri)
    tri_col = mask.astype(F32)
    tri_row = ((ci <= ri) if reverse else (ri <= ci)).astype(F32)

    li_col = jnp.concatenate([gc_ref[b][:, off:off + nh] for b in range(nb)], axis=1)
    lf_col = _log_sigmoid(jnp.concatenate([gc_ref[b][:, off + nh:off + 2 * nh] for b in range(nb)], axis=1))
    li_row = jnp.concatenate([gr_ref[b][off:off + nh, :] for b in range(nb)], axis=0)
    lf_row = _log_sigmoid(jnp.concatenate([gr_ref[b][off + nh:off + 2 * nh, :] for b in range(nb)], axis=0))
    b_col = jnp.dot(tri_col, lf_col, precision=HIGHEST, preferred_element_type=F32)
    b_row = jnp.dot(lf_row, tri_row, precision=HIGHEST, preferred_element_type=F32)
    beta_col = li_col - b_col
    beta_row = li_row - b_row
    cmx = beta_col
    rowi = lax.broadcasted_iota(jnp.int32, (ch, nq), 0)
    k = 1
    while k < ch:
        if reverse:
            cmx = jnp.maximum(cmx, jnp.where(rowi < ch - k, pltpu.roll(cmx, ch - k, 0), -jnp.inf))
        else:
            cmx = jnp.maximum(cmx, jnp.where(rowi >= k, pltpu.roll(cmx, k, 0), -jnp.inf))
        k *= 2
    last = 0 if reverse else ch - 1
    g_row = b_col[last:last + 1, :]
    m_loc = g_row + cmx[last:last + 1, :]
    m_prev = m_s[slot:slot + 1, 0:nq]
    mx = jnp.maximum(m_prev, cmx)
    s_inter = jnp.exp(m_prev - mx)
    inv_floor = jnp.exp(-mx - b_col)
    m_new = jnp.maximum(g_row + m_prev, m_loc)
    s_old = jnp.exp(g_row + m_prev - m_new)
    s_new = jnp.exp(m_loc - m_new)
    w_row = jnp.exp(beta_row - jnp.max(beta_row, axis=1, keepdims=True))
    m_s[slot:slot + 1, 0:nq] = m_new

    ones_col = (lax.broadcasted_iota(jnp.int32, (ch, HEAD_DIM), 1) == 0).astype(BF16)
    for b in range(nb):
        for h in range(nh):
            i = b * nh + h
            hs = slice(h * HEAD_DIM, (h + 1) * HEAD_DIM)
            q = q_ref[b, :, hs]
            kt = kt_ref[b, hs, :]
            v_aug = jnp.concatenate([v_ref[b, :, hs], ones_col], axis=1)
            d_mat = jnp.exp(jnp.where(mask, beta_row[i:i + 1, :] - mx[:, i:i + 1], -jnp.inf))
            scores = (_dot(q, kt) * d_mat).astype(BF16)
            c_prev = c_s[slot * nq + i]
            num = _dot(scores, v_aug) + s_inter[:, i:i + 1] * _dot(q, c_prev.astype(BF16))
            den = num[:, HEAD_DIM:HEAD_DIM + 1]
            out_ref[b, :, hs] = num[:, :HEAD_DIM] * (1.0 / jnp.maximum(jnp.abs(den), inv_floor[:, i:i + 1]))
            ktw = (kt.astype(F32) * w_row[i:i + 1, :]).astype(BF16)
            c_s[slot * nq + i] = s_old[:, i:i + 1] * c_prev + s_new[:, i:i + 1] * _dot(ktw, v_aug)


def _mlstm_kernel(qf, ktf, vf, gcf, grf, qb, ktb, vb, gcb, grb, of, ob, c_s, m_s):
    @pl.when(pl.program_id(0) == 0)
    def _():
        c_s[...] = jnp.zeros_like(c_s)
        m_s[...] = jnp.zeros_like(m_s)

    _mlstm_dir(False, qf, ktf, vf, gcf, grf, of, c_s, m_s, 0)
    _mlstm_dir(True, qb, ktb, vb, gcb, grb, ob, c_s, m_s, 1)


def _mlstm(q, kt, v, g_col, g_row):
    B, L, dm = q.shape
    ch = min(CHUNK, L)
    nc = L // ch
    specs = []
    for cidx in ((lambda c: c), (lambda c: nc - 1 - c)):
        tok = lambda w, cidx=cidx: pl.BlockSpec((B, ch, w), lambda c: (0, cidx(c), 0))
        tr = lambda w, cidx=cidx: pl.BlockSpec((B, w, ch), lambda c: (0, 0, cidx(c)))
        specs.append(([tok(dm), tr(dm), tok(dm), tok(g_col.shape[2]), tr(g_row.shape[1])], tok(dm)))
    args = [q, kt, v, g_col, g_row]
    out = jax.ShapeDtypeStruct((B, L, dm), F32)
    return pl.pallas_call(
        _mlstm_kernel, grid=(nc,), in_specs=specs[0][0] + specs[1][0],
        out_specs=(specs[0][1], specs[1][1]), out_shape=(out, out),
        scratch_shapes=[pltpu.VMEM((2 * B * MLSTM_HEADS, HEAD_DIM, 2 * HEAD_DIM), F32),
                        pltpu.VMEM((SUBLANES, 128), F32)],
        compiler_params=_params("arbitrary"), name="mlstm")(*args, *args)


def _filter_dft1_kernel(seq_len, m_ref, fvec_ref, ph_ref, w1h_ref, w1l_ref, b1_ref, w2h_ref, w2l_ref,
                        b2_ref, w3h_ref, w3l_ref, b3_ref, fr_ref, wfh_ref, wfl_ref, dec_ref,
                        out_ref, l1_ref, cb_s, sb_s):
    i = pl.program_id(0)
    nt = out_ref.shape[0]
    half = m_ref.shape[1] // 2
    hid = FILTER_HIDDEN
    C = dec_ref.shape[1]
    L = seq_len
    lane = lax.broadcasted_iota(jnp.int32, (half, 2 * hid), 1)
    grp = lane >= hid
    feat = lane & (hid - 1)
    r2 = lax.broadcasted_iota(jnp.int32, (half, 2 * hid), 0)
    r_col = lax.broadcasted_iota(jnp.int32, (half, 1), 0)
    fvec = fvec_ref[...]
    fr = fr_ref[...]
    sign = jnp.where(grp[0:1], -1.0, 1.0)
    scale = (2.0 * math.pi) / L

    @pl.when(i == 0)
    def _():
        l1_ref[...] = jnp.zeros_like(l1_ref)
        lag0 = DFT_N1 * jnp.where(grp, half - r2, r2)
        ang = fvec * (lag0.astype(F32) * scale)
        cb_s[...] = jnp.cos(ang)
        sb_s[...] = jnp.sin(ang)

    def lag(n):
        return jnp.where(n < L, n, jnp.where(n == L, 0, 2 * L - n)).astype(F32)

    def body(jn, carry):
        n1 = i * nt + jn
        n = n1 + DFT_N1 * (r2 + jnp.where(grp, half, 0))
        ang1 = sign * (fvec * (n1.astype(F32) * scale)) + ph_ref[...]
        feats = jnp.cos(ang1) * cb_s[...] - jnp.sin(ang1) * sb_s[...]
        feats = jnp.where(n == L, jnp.cos(ph_ref[...]), feats)
        z = jnp.where(feat == 0, lag(n) / (L - 1), jnp.where(feat <= 2 * HYENA_BANDS, feats, 0.0))
        h = jnp.sin(fr * (_dot3(z, w1h_ref[...], w1l_ref[...]) + b1_ref[...]))
        h = jnp.sin(fr * (_dot3(h, w2h_ref[...], w2l_ref[...]) + b2_ref[...]))
        h = jnp.sin(fr * (_dot3(h, w3h_ref[...], w3l_ref[...]) + b3_ref[...]))
        taps = []
        for gi in range(2):
            cs = slice(gi * C, (gi + 1) * C)
            hg = jnp.where(grp == (gi == 1), h, 0.0)
            n_col = n1 + DFT_N1 * (r_col + gi * half)
            tp = (_dot3(hg, wfh_ref[:, cs], wfl_ref[:, cs])
                  * jnp.exp(-(lag(n_col) / (L - 1)) * jnp.abs(dec_ref[gi:gi + 1, :])))
            l1_ref[...] += jnp.sum(jnp.abs(tp), axis=0, keepdims=True)
            taps.append(jnp.where(n_col == L, 0.0, tp).astype(BF16))
        out_ref[jn] = _dot(m_ref[...], jnp.concatenate(taps, axis=0))
        return carry

    lax.fori_loop(0, nt, body, 0)


def _blockdiag2(w):
    z = jnp.zeros_like(w)
    return jnp.concatenate([jnp.concatenate([w, z], axis=1), jnp.concatenate([z, w], axis=1)], axis=0)


def _filter_dft1(L, m1, w1, b1, w2, b2, w3, b3, fr, wfo, dec):
    C = wfo.shape[1] // 2
    hid = FILTER_HIDDEN
    n2 = m1.shape[1]
    fgrid = jnp.linspace(1e-4, HYENA_BANDS - 1, HYENA_BANDS, dtype=F32)
    fhalf = jnp.zeros((hid,), F32).at[1:1 + HYENA_BANDS].set(fgrid).at[1 + HYENA_BANDS:1 + 2 * HYENA_BANDS].set(fgrid)
    phalf = jnp.zeros((hid,), F32).at[1 + HYENA_BANDS:1 + 2 * HYENA_BANDS].set(0.5 * math.pi)
    two = lambda a: jnp.concatenate([a.reshape(1, -1), a.reshape(1, -1)], axis=1)
    w1p = jnp.zeros((hid, hid), F32).at[:w1.shape[0]].set(w1)
    mats = []
    for wmat in (_blockdiag2(w1p), _blockdiag2(w2), _blockdiag2(w3)):
        mats.append(_split(wmat))
    wst = jnp.concatenate([wfo, wfo], axis=0)
    wfh, wfl = _split(wst)
    args = (m1, two(fhalf), two(phalf), mats[0][0], mats[0][1], two(b1), mats[1][0], mats[1][1], two(b2),
            mats[2][0], mats[2][1], two(b3), two(fr), wfh, wfl, dec)
    full = lambda a: pl.BlockSpec(a.shape, lambda i: (0,) * a.ndim)
    nt = SUBLANES
    return pl.pallas_call(
        functools.partial(_filter_dft1_kernel, L),
        grid=(DFT_N1 // nt,), in_specs=[full(a) for a in args],
        out_specs=(pl.BlockSpec((nt, 2 * n2, C), lambda i: (i, 0, 0)), pl.BlockSpec((1, C), lambda i: (0, 0))),
        out_shape=(jax.ShapeDtypeStruct((DFT_N1, 2 * n2, C), F32), jax.ShapeDtypeStruct((1, C), F32)),
        scratch_shapes=[pltpu.VMEM((n2 // 2, 2 * hid), F32), pltpu.VMEM((n2 // 2, 2 * hid), F32)],
        compiler_params=_params("arbitrary"),
        name="filter_dft1")(*args)


@functools.lru_cache(maxsize=None)
def _dft_tables(n1, n2):
    n = n1 * n2
    h = n2 // 2
    a2 = 2.0 * np.pi * np.outer(np.arange(n2), np.arange(n2)) / n2
    c2, s2 = np.cos(a2), np.sin(a2)
    m1d = np.empty((n2, 2, n2))
    m1d[:, 0, :h], m1d[:, 0, h:] = c2[:, :h], s2[:, :h]
    m1d[:, 1, :h], m1d[:, 1, h:] = -s2[:, :h], c2[:, :h]
    m1f = np.stack([c2, -s2], axis=1)
    a1 = 2.0 * np.pi * np.outer(np.arange(n1), np.arange(n1)) / n1
    f1r, f1i = np.cos(a1), -np.sin(a1)
    pg = np.block([[f1r, -f1i], [f1i, f1r]])
    pgs = np.block([[-f1i, -f1r], [f1r, -f1i]])
    qd = np.block([[f1r, f1i], [-f1i, f1r]])
    at = 2.0 * np.pi * np.outer(np.arange(n2), np.arange(n1)) / n
    tr, ti = np.cos(at), -np.sin(at)
    txr = np.concatenate([tr, tr], axis=1)
    txi = np.concatenate([ti, ti], axis=1)
    er, ei = c2[:h] / n, s2[:h] / n
    pp = np.block([[er, -ei], [ei, er]])
    pps = np.block([[-ei, -er], [er, -ei]])
    ur, ui = np.cos(at).T, np.sin(at).T
    uxr = np.concatenate([ur, ur], axis=1)
    uxi = np.concatenate([ui, ui], axis=1)
    f32 = lambda a: np.asarray(a, np.float32)
    return dict(m1d=f32(m1d.reshape(2 * n2, n2)), m1f=f32(m1f.reshape(2 * n2, n2)),
                pg=f32(pg), pgs=f32(pgs), qd=f32(qd), txr=f32(txr), txi=f32(txi),
                pp=f32(pp), pps=f32(pps), uxr=f32(uxr), uxi=f32(uxi))


def _dft1_kernel(m_ref, x_ref, out_ref):
    nt = out_ref.shape[0]
    m = m_ref[...]
    for j in range(nt):
        parts = [x_ref[b, :, j, :] for b in range(x_ref.shape[0])]
        x = parts[0] if len(parts) == 1 else jnp.concatenate(parts, axis=0)
        out_ref[j] = _dot(m, x.astype(BF16))


def _dft1(m1, x):
    parts, rows, n1, C = x.shape
    nt = SUBLANES
    return pl.pallas_call(
        _dft1_kernel, grid=(n1 // nt,),
        in_specs=[pl.BlockSpec(m1.shape, lambda i: (0, 0)),
                  pl.BlockSpec((parts, rows, nt, C), lambda i: (0, 0, i, 0))],
        out_specs=pl.BlockSpec((nt, m1.shape[0], C), lambda i: (i, 0, 0)),
        out_shape=jax.ShapeDtypeStruct((n1, m1.shape[0], C), F32),
        compiler_params=_params("parallel"), name="dft_stage1")(m1, x)


def _stage2(a_ref, pg_ref, pgs_ref, txr_ref, txi_ref, j):
    gd = (pg_ref[...] * txr_ref[j:j + 1, :] + pgs_ref[...] * txi_ref[j:j + 1, :]).astype(BF16)
    x = jnp.concatenate([a_ref[:, 2 * j, :], a_ref[:, 2 * j + 1, :]], axis=0).astype(BF16)
    return _dot(gd, x)


def _spectrum_kernel(a_ref, pg_ref, pgs_ref, txr_ref, txi_ref, l1_ref, kf_ref):
    inv = 1.0 / l1_ref[...]
    for j in range(kf_ref.shape[0]):
        x = _stage2(a_ref, pg_ref, pgs_ref, txr_ref, txi_ref, j)
        kf_ref[j] = (x * inv).astype(kf_ref.dtype).reshape(kf_ref.shape[1:])


def _conv_mid_kernel(a_ref, kf_ref, pg_ref, pgs_ref, txr_ref, txi_ref, qd_ref, out_ref):
    n1 = a_ref.shape[0]
    qd = qd_ref[...]
    for j in range(out_ref.shape[0]):
        x = _stage2(a_ref, pg_ref, pgs_ref, txr_ref, txi_ref, j)
        xr, xi = x[:n1], x[n1:]
        kr, ki = kf_ref[j, 0].astype(F32), kf_ref[j, 1].astype(F32)
        y = jnp.concatenate([xr * kr - xi * ki, xr * ki + xi * kr], axis=0).astype(BF16)
        out_ref[j] = _dot(qd, y).reshape(out_ref.shape[1:])


def _dft_mid(a, tabs, l1=None, kf=None):
    n1, rows, C = a.shape
    n2 = rows // 2
    kt = SUBLANES
    ablk = pl.BlockSpec((n1, 2 * kt, C), lambda i: (0, i, 0))
    blk = pl.BlockSpec((kt, 2, n1, C), lambda i: (i, 0, 0, 0))
    mat = pl.BlockSpec((2 * n1, 2 * n1), lambda i: (0, 0))
    tw = pl.BlockSpec((kt, 2 * n1), lambda i: (i, 0))
    if kf is None:
        return pl.pallas_call(
            _spectrum_kernel, grid=(n2 // kt,),
            in_specs=[ablk, mat, mat, tw, tw, pl.BlockSpec((1, C), lambda i: (0, 0))],
            out_specs=blk, out_shape=jax.ShapeDtypeStruct((n2, 2, n1, C), BF16),
            compiler_params=_params("parallel"), name="filter_spectrum")(
                a, tabs["pg"], tabs["pgs"], tabs["txr"], tabs["txi"], l1)
    return pl.pallas_call(
        _conv_mid_kernel, grid=(n2 // kt,),
        in_specs=[ablk, blk, mat, mat, tw, tw, mat],
        out_specs=blk, out_shape=jax.ShapeDtypeStruct((n2, 2, n1, C), F32),
        compiler_params=_params("parallel"), name="conv_mid")(
            a, kf, tabs["pg"], tabs["pgs"], tabs["txr"], tabs["txi"], tabs["qd"])


def _conv_out_kernel(b_ref, pp_ref, pps_ref, uxr_ref, uxi_ref, out_ref):
    h = out_ref.shape[1]
    for j in range(out_ref.shape[2]):
        hm = (pp_ref[...] * uxr_ref[j:j + 1, :] + pps_ref[...] * uxi_ref[j:j + 1, :]).astype(BF16)
        bcat = jnp.concatenate([b_ref[:, 0, j, :], b_ref[:, 1, j, :]], axis=0).astype(BF16)
        y = _dot(hm, bcat)
        out_ref[0, :, j, :] = y[:h]
        out_ref[1, :, j, :] = y[h:]


def _conv_out(bm, tabs):
    n2, _, n1, C = bm.shape
    nt = SUBLANES
    mat = pl.BlockSpec((n2, 2 * n2), lambda i: (0, 0))
    tw = pl.BlockSpec((nt, 2 * n2), lambda i: (i, 0))
    return pl.pallas_call(
        _conv_out_kernel, grid=(n1 // nt,),
        in_specs=[pl.BlockSpec((n2, 2, nt, C), lambda i: (0, 0, i, 0)), mat, mat, tw, tw],
        out_specs=pl.BlockSpec((2, n2 // 2, nt, C), lambda i: (0, 0, i, 0)),
        out_shape=jax.ShapeDtypeStruct((2, n2 // 2, n1, C), F32),
        compiler_params=_params("parallel"), name="conv_out")(
            bm, tabs["pp"], tabs["pps"], tabs["uxr"], tabs["uxi"])


def _long_conv(zc, filt):
    B, L, C = zc.shape
    assert B == 2, "the two batch rows are packed as one complex sequence"
    n1 = DFT_N1
    n2 = 2 * L // n1
    tabs = {k: jnp.asarray(v) for k, v in _dft_tables(n1, n2).items()}
    af, l1 = _filter_dft1(L, tabs["m1f"].astype(BF16), *filt)
    kf = _dft_mid(af, tabs, l1=l1)
    a = _dft1(tabs["m1d"].astype(BF16), zc.reshape(B, n2 // 2, n1, C))
    bm = _dft_mid(a, tabs, kf=kf)
    return _conv_out(bm, tabs).reshape(B, L, C)


def _merge_kernel(x_ref, hf_ref, hb_ref, o_ref, y_ref, zc_ref, x0_ref, hn_ref, skip_ref, g_ref,
                  wga_ref, bga_ref, wgb_ref, bgb_ref, wa_ref, wb_ref, wo_ref, out_ref):
    x = x_ref[0]
    n = _rms(x, g_ref[...]).astype(BF16)
    ga = jax.nn.sigmoid(_dot(n, wga_ref[...]) + bga_ref[...])
    gb = jax.nn.sigmoid(_dot(n, wgb_ref[...]) + bgb_ref[...])
    heads = []
    for h in range(MLSTM_HEADS):
        hs = slice(h * HEAD_DIM, (h + 1) * HEAD_DIM)
        s = hf_ref[0, :, hs] + hb_ref[0, :, hs]
        dv = s - jnp.mean(s, axis=1, keepdims=True)
        var = jnp.mean(dv * dv, axis=1, keepdims=True)
        heads.append((dv * lax.rsqrt(var + EPS) * hn_ref[:, hs]
                      * jax.nn.sigmoid(o_ref[0, :, hs])).astype(BF16))
    ml = jnp.concatenate(heads, axis=1)
    hy = (x0_ref[0] * (y_ref[0] + skip_ref[...] * zc_ref[0])).astype(BF16)
    merged = ga * _dot(ml, wa_ref[...]) + gb * _dot(hy, wb_ref[...])
    out_ref[0] = x + _dot(merged.astype(BF16), wo_ref[...])


def _merge(x, hf, hb, o, y, zc, x0, hn, skip, g, wga, bga, wgb, bgb, wa, wb, wo):
    B, L, D = x.shape
    tm = min(TOKEN_BLOCK, L)
    full = lambda w: pl.BlockSpec(w.shape, lambda bb, i: (0,) * w.ndim)
    tok = lambda w: pl.BlockSpec((1, tm, w), lambda bb, i: (bb, i, 0))
    ws = (hn, skip, g, wga, bga, wgb, bgb, wa, wb, wo)
    acts = (hf, hb, o, y, zc, x0)
    return pl.pallas_call(
        _merge_kernel, grid=(B, L // tm),
        in_specs=[tok(D)] + [tok(a.shape[2]) for a in acts] + [full(w) for w in ws],
        out_specs=tok(D), out_shape=jax.ShapeDtypeStruct(x.shape, F32),
        compiler_params=_params("parallel", "parallel"), name="merge")(x, *acts, *ws)


def _mlp_kernel(final, nchunk, x_ref, g_ref, w1_ref, w2_ref, gf_ref, out_ref):
    x = x_ref[0]
    n = _rms(x, g_ref[...]).astype(BF16)
    fc = w1_ref.shape[1] // nchunk
    acc = x
    for c in range(nchunk):
        hcl = jnp.maximum(_dot(n, w1_ref[:, c * fc:(c + 1) * fc]), 0.0)
        acc = acc + _dot((hcl * hcl).astype(BF16), w2_ref[c * fc:(c + 1) * fc, :])
    out_ref[0] = _rms(acc, gf_ref[...]) if final else acc


def _mlp(x, g, w1, w2, gf, final):
    B, L, D = x.shape
    tm = min(TOKEN_BLOCK, L)
    full = lambda w: pl.BlockSpec(w.shape, lambda bb, i: (0,) * w.ndim)
    tok = pl.BlockSpec((1, tm, D), lambda bb, i: (bb, i, 0))
    return pl.pallas_call(
        functools.partial(_mlp_kernel, final, 4), grid=(B, L // tm),
        in_specs=[tok, full(g), full(w1), full(w2), full(gf)],
        out_specs=tok, out_shape=jax.ShapeDtypeStruct(x.shape, F32),
        compiler_params=_params("parallel", "parallel"), name="mlp")(x, g, w1, w2, gf)


def _trunk(x, p):
    depth = p["w_in"].shape[0]
    D = x.shape[2]
    L = x.shape[1]
    dm = HEAD_DIM * MLSTM_HEADS
    ng = 4 * MLSTM_HEADS
    dh = p["hy_skip"].shape[1]
    o_g, o_u, o_ga = 4 * dm, 4 * dm + ng, 4 * dm + ng + 3 * dh
    row = lambda a: a.reshape(1, -1)
    col = lambda a: a.reshape(-1, 1)
    for l in range(depth):
        w_in = p["w_in"][l]
        b_in = p["b_in"][l]
        w_qvo = jnp.concatenate([w_in[:, :dm], w_in[:, 2 * dm:4 * dm]], axis=1)
        b_qvo = jnp.concatenate([b_in[:dm], b_in[2 * dm:4 * dm]])
        w_g = w_in[:, o_g:o_u]
        q, kt, v, o, g_col, g_row, x0, zc = _inproj(
            x, row(p["norm_mix"][l]), w_qvo.astype(BF16), row(b_qvo),
            w_in[:, dm:2 * dm].T.astype(BF16), col(b_in[dm:2 * dm]),
            w_g.astype(BF16), row(b_in[o_g:o_u]), w_g.T.astype(BF16), col(b_in[o_g:o_u]),
            w_in[:, o_u:o_ga].astype(BF16), row(b_in[o_u:o_ga]),
            p["hy_conv_w"][l], row(p["hy_conv_b"][l]))
        h_fwd, h_bwd = _mlstm(q, kt, v, g_col, g_row)

        y = _long_conv(zc, (p["hy_w1"][l], p["hy_b1"][l], p["hy_w2"][l], p["hy_b2"][l], p["hy_w3"][l],
                            p["hy_b3"][l], p["hy_freq"][l], p["hy_w_fo"][l], p["hy_decay"][l]))

        x = _merge(x, h_fwd, h_bwd, o, y, zc, x0, row(p["mlstm_norm"][l]), row(p["hy_skip"][l]),
                   row(p["norm_mix"][l]),
                   w_in[:, o_ga:o_ga + D].astype(BF16), row(b_in[o_ga:o_ga + D]),
                   w_in[:, o_ga + D:].astype(BF16), row(b_in[o_ga + D:]),
                   p["w_branch_a"][l].astype(BF16), p["w_branch_b"][l].astype(BF16),
                   p["w_out"][l].astype(BF16))
        x = _mlp(x, row(p["norm_mlp"][l]), p["w_mlp1"][l].astype(BF16), p["w_mlp2"][l].astype(BF16),
                 row(p["norm_final"]), l == depth - 1)
    return x


def kernel(x_prompt, x_sample, norm_mix, w_in, b_in, mlstm_norm, hy_conv_w, hy_conv_b, hy_w1, hy_b1, hy_w2, hy_b2, hy_w3, hy_b3, hy_freq, hy_w_fo, hy_decay, hy_skip, w_branch_a, w_branch_b, w_out, norm_mlp, w_mlp1, w_mlp2, norm_final):
    p = dict(norm_mix=norm_mix, w_in=w_in, b_in=b_in, mlstm_norm=mlstm_norm, hy_conv_w=hy_conv_w,
             hy_conv_b=hy_conv_b, hy_w1=hy_w1, hy_b1=hy_b1, hy_w2=hy_w2, hy_b2=hy_b2, hy_w3=hy_w3,
             hy_b3=hy_b3, hy_freq=hy_freq, hy_w_fo=hy_w_fo, hy_decay=hy_decay, hy_skip=hy_skip,
             w_branch_a=w_branch_a, w_branch_b=w_branch_b, w_out=w_out, norm_mlp=norm_mlp,
             w_mlp1=w_mlp1, w_mlp2=w_mlp2, norm_final=norm_final)
    return (_trunk(x_prompt, p), _trunk(x_sample, p))
```

```python
import functools
import math

import numpy as np
import jax
import jax.numpy as jnp
from jax import lax
from jax.experimental import pallas as pl
from jax.experimental.pallas import tpu as pltpu

F32 = jnp.float32
BF16 = jnp.bfloat16

EPS = 1e-6
MLSTM_HEADS = 4
HEAD_DIM = 128
CHUNK = 256
HYENA_BANDS = 16
FILTER_HIDDEN = 64
DFT_N1 = 128
FILTER_ROWS = 512
SUBLANES = 8
TOKEN_BLOCK = 512
VMEM_LIMIT = 48 * 1024 * 1024
HIGHEST = lax.Precision.HIGHEST


def _params(*sem):
    return pltpu.CompilerParams(dimension_semantics=sem, vmem_limit_bytes=VMEM_LIMIT)


def _rms(x, g):
    return x * lax.rsqrt(jnp.mean(x * x, axis=-1, keepdims=True) + EPS) * g


def _dot(a, b):
    return jnp.dot(a, b, preferred_element_type=F32)


def _dot_nt(a, b):
    return lax.dot_general(a, b, (((1,), (1,)), ((), ())), preferred_element_type=F32)


def _split(a):
    hi = a.astype(BF16)
    return hi, (a - hi.astype(F32)).astype(BF16)


def _dot3(a, b_hi, b_lo):
    a_hi, a_lo = _split(a)
    return _dot(a_hi, b_hi) + (_dot(a_hi, b_lo) + _dot(a_lo, b_hi))


def _inproj_kernel(x_ref, xp_ref, xn_ref, g_ref, wm_ref, bm_ref, wkt_ref, bk_ref,
                   wgt_ref, bgt_ref, wu_ref, bu_ref, cw_ref, cb_ref,
                   q_ref, kt_ref, v_ref, o_ref, gc_ref, gr_ref, x0_ref, zc_ref):
    i = pl.program_id(1)
    last = pl.num_programs(1) - 1
    g = g_ref[...]
    n = _rms(x_ref[0], g).astype(BF16)
    dm = HEAD_DIM * MLSTM_HEADS
    pm = _dot(n, wm_ref[...]) + bm_ref[...]
    q_ref[0] = (pm[:, :dm] * (HEAD_DIM ** -0.5)).astype(BF16)
    v_ref[0] = pm[:, dm:2 * dm].astype(BF16)
    o_ref[0] = pm[:, 2 * dm:]
    kt_ref[0] = (_dot_nt(wkt_ref[...], n) + bk_ref[...]).astype(BF16)
    gates = _dot_nt(wgt_ref[...], n) + bgt_ref[...]
    gr_ref[0] = gates
    gc_ref[0] = gates.T

    tm = n.shape[0]
    halo = _rms(jnp.concatenate([xp_ref[0], xn_ref[0]], axis=0), g).astype(BF16)
    u_all = _dot(jnp.concatenate([n, halo], axis=0), wu_ref[...]) + bu_ref[...]
    u = u_all[:tm]
    u_prev = jnp.where(i > 0, u_all[tm + SUBLANES - 1:tm + SUBLANES], 0.0)
    u_next = jnp.where(i < last, u_all[tm + SUBLANES:tm + SUBLANES + 1], 0.0)
    row = lax.broadcasted_iota(jnp.int32, (tm, 1), 0)
    u_m1 = jnp.where(row == 0, u_prev, pltpu.roll(u, 1, 0))
    u_p1 = jnp.where(row == tm - 1, u_next, pltpu.roll(u, tm - 1, 0))
    cw = cw_ref[...]
    c = u_m1 * cw[0:1] + u * cw[1:2] + u_p1 * cw[2:3] + cb_ref[...]
    dh = c.shape[1] // 3
    x0_ref[0] = c[:, :dh]
    zc_ref[0] = c[:, dh:2 * dh] * c[:, 2 * dh:]


def _inproj(x, g, wm, bm, wkt, bk, wgt, bgt, wu, bu, cw, cb):
    B, L, D = x.shape
    tm = min(TOKEN_BLOCK, L)
    nb = L // tm
    r8 = tm // SUBLANES
    dm = wkt.shape[0]
    dh = wu.shape[1] // 3
    ng = wgt.shape[0]
    ws = (g, wm, bm, wkt, bk, wgt, bgt, wu, bu, cw, cb)
    full = lambda a: pl.BlockSpec(a.shape, lambda b, i: (0,) * a.ndim)
    tok = lambda w: pl.BlockSpec((1, tm, w), lambda b, i: (b, i, 0))
    tr = lambda w: pl.BlockSpec((1, w, tm), lambda b, i: (b, 0, i))
    in_specs = [
        tok(D),
        pl.BlockSpec((1, SUBLANES, D), lambda b, i: (b, jnp.maximum(i * r8 - 1, 0), 0)),
        pl.BlockSpec((1, SUBLANES, D), lambda b, i: (b, jnp.minimum((i + 1) * r8, L // SUBLANES - 1), 0)),
    ] + [full(w) for w in ws]
    sds = jax.ShapeDtypeStruct
    out_shape = (sds((B, L, dm), BF16), sds((B, dm, L), BF16), sds((B, L, dm), BF16), sds((B, L, dm), F32),
                 sds((B, L, ng), F32), sds((B, ng, L), F32), sds((B, L, dh), F32), sds((B, L, dh), F32))
    out_specs = (tok(dm), tr(dm), tok(dm), tok(dm), tok(ng), tr(ng), tok(dh), tok(dh))
    return pl.pallas_call(
        _inproj_kernel, grid=(B, nb), in_specs=in_specs, out_specs=out_specs,
        out_shape=out_shape, compiler_params=_params("parallel", "parallel"),
        name="inproj")(x, x, x, *ws)


def _log_sigmoid(x):
    return jnp.minimum(x, 0.0) - jnp.log1p(jnp.exp(-jnp.abs(x)))


def _mlstm_dir(reverse, q_ref, kt_ref, v_ref, gc_ref, gr_ref, out_ref, c_s, m_s, slot):
    nh = MLSTM_HEADS
    nb = q_ref.shape[0]
    nq = nb * nh
    ch = q_ref.shape[1]
    off = 2 * nh if reverse else 0
    ri = lax.broadcasted_iota(jnp.int32, (ch, ch), 0)
    ci = lax.broadcasted_iota(jnp.int32, (ch, ch), 1)
    mask = (ri <= ci) if reverse else (ci <= ri)
    tri_col = mask.astype(F32)
    tri_row = ((ci <= ri) if reverse else (ri <= ci)).astype(F32)

    li_col = jnp.concatenate([gc_ref[b][:, off:off + nh] for b in range(nb)], axis=1)
    lf_col = _log_sigmoid(jnp.concatenate([gc_ref[b][:, off + nh:off + 2 * nh] for b in range(nb)], axis=1))
    li_row = jnp.concatenate([gr_ref[b][off:off + nh, :] for b in range(nb)], axis=0)
    lf_row = _log_sigmoid(jnp.concatenate([gr_ref[b][off + nh:off + 2 * nh, :] for b in range(nb)], axis=0))
    b_col = jnp.dot(tri_col, lf_col, precision=HIGHEST, preferred_element_type=F32)
    b_row = jnp.dot(lf_row, tri_row, precision=HIGHEST, preferred_element_type=F32)
    beta_col = li_col - b_col
    beta_row = li_row - b_row
    cmx = beta_col
    rowi = lax.broadcasted_iota(jnp.int32, (ch, nq), 0)
    k = 1
    while k < ch:
        if reverse:
            cmx = jnp.maximum(cmx, jnp.where(rowi < ch - k, pltpu.roll(cmx, ch - k, 0), -jnp.inf))
        else:
            cmx = jnp.maximum(cmx, jnp.where(rowi >= k, pltpu.roll(cmx, k, 0), -jnp.inf))
        k *= 2
    last = 0 if reverse else ch - 1
    g_row = b_col[last:last + 1, :]
    m_loc = g_row + cmx[last:last + 1, :]
    m_prev = m_s[slot:slot + 1, 0:nq]
    mx = jnp.maximum(m_prev, cmx)
    s_inter = jnp.exp(m_prev - mx)
    inv_floor = jnp.exp(-mx - b_col)
    m_new = jnp.maximum(g_row + m_prev, m_loc)
    s_old = jnp.exp(g_row + m_prev - m_new)
    s_new = jnp.exp(m_loc - m_new)
    w_row = jnp.exp(beta_row - jnp.max(beta_row, axis=1, keepdims=True))
    m_s[slot:slot + 1, 0:nq] = m_new

    ones_col = (lax.broadcasted_iota(jnp.int32, (ch, HEAD_DIM), 1) == 0).astype(BF16)
    for b in range(nb):
        for h in range(nh):
            i = b * nh + h
            hs = slice(h * HEAD_DIM, (h + 1) * HEAD_DIM)
            q = q_ref[b, :, hs]
            kt = kt_ref[b, hs, :]
            v_aug = jnp.concatenate([v_ref[b, :, hs], ones_col], axis=1)
            d_mat = jnp.exp(jnp.where(mask, beta_row[i:i + 1, :] - mx[:, i:i + 1], -jnp.inf))
            scores = (_dot(q, kt) * d_mat).astype(BF16)
            c_prev = c_s[slot * nq + i]
            num = _dot(scores, v_aug) + s_inter[:, i:i + 1] * _dot(q, c_prev.astype(BF16))
            den = num[:, HEAD_DIM:HEAD_DIM + 1]
            out_ref[b, :, hs] = num[:, :HEAD_DIM] * (1.0 / jnp.maximum(jnp.abs(den), inv_floor[:, i:i + 1]))
            ktw = (kt.astype(F32) * w_row[i:i + 1, :]).astype(BF16)
            c_s[slot * nq + i] = s_old[:, i:i + 1] * c_prev + s_new[:, i:i + 1] * _dot(ktw, v_aug)


def _mlstm_kernel(qf, ktf, vf, gcf, grf, qb, ktb, vb, gcb, grb, of, ob, c_s, m_s):
    @pl.when(pl.program_id(0) == 0)
    def _():
        c_s[...] = jnp.zeros_like(c_s)
        m_s[...] = jnp.zeros_like(m_s)

    _mlstm_dir(False, qf, ktf, vf, gcf, grf, of, c_s, m_s, 0)
    _mlstm_dir(True, qb, ktb, vb, gcb, grb, ob, c_s, m_s, 1)


def _mlstm(q, kt, v, g_col, g_row):
    B, L, dm = q.shape
    ch = min(CHUNK, L)
    nc = L // ch
    specs = []
    for cidx in ((lambda c: c), (lambda c: nc - 1 - c)):
        tok = lambda w, cidx=cidx: pl.BlockSpec((B, ch, w), lambda c: (0, cidx(c), 0))
        tr = lambda w, cidx=cidx: pl.BlockSpec((B, w, ch), lambda c: (0, 0, cidx(c)))
        specs.append(([tok(dm), tr(dm), tok(dm), tok(g_col.shape[2]), tr(g_row.shape[1])], tok(dm)))
    args = [q, kt, v, g_col, g_row]
    out = jax.ShapeDtypeStruct((B, L, dm), F32)
    return pl.pallas_call(
        _mlstm_kernel, grid=(nc,), in_specs=specs[0][0] + specs[1][0],
        out_specs=(specs[0][1], specs[1][1]), out_shape=(out, out),
        scratch_shapes=[pltpu.VMEM((2 * B * MLSTM_HEADS, HEAD_DIM, 2 * HEAD_DIM), F32),
                        pltpu.VMEM((SUBLANES, 128), F32)],
        compiler_params=_params("arbitrary"), name="mlstm")(*args, *args)


def _filter_dft1_kernel(seq_len, m_ref, fvec_ref, ph_ref, w1h_ref, w1l_ref, b1_ref, w2h_ref, w2l_ref,
                        b2_ref, w3h_ref, w3l_ref, b3_ref, fr_ref, wfh_ref, wfl_ref, dec_ref,
                        out_ref, l1_ref, cb_s, sb_s):
    i = pl.program_id(0)
    nt = out_ref.shape[0]
    half = m_ref.shape[1] // 2
    hid = FILTER_HIDDEN
    C = dec_ref.shape[1]
    L = seq_len
    lane = lax.broadcasted_iota(jnp.int32, (half, 2 * hid), 1)
    grp = lane >= hid
    feat = lane & (hid - 1)
    r2 = lax.broadcasted_iota(jnp.int32, (half, 2 * hid), 0)
    r_col = lax.broadcasted_iota(jnp.int32, (half, 1), 0)
    fvec = fvec_ref[...]
    fr = fr_ref[...]
    sign = jnp.where(grp[0:1], -1.0, 1.0)
    scale = (2.0 * math.pi) / L

    @pl.when(i == 0)
    def _():
        l1_ref[...] = jnp.zeros_like(l1_ref)
        lag0 = DFT_N1 * jnp.where(grp, half - r2, r2)
        ang = fvec * (lag0.astype(F32) * scale)
        cb_s[...] = jnp.cos(ang)
        sb_s[...] = jnp.sin(ang)

    def lag(n):
        return jnp.where(n < L, n, jnp.where(n == L, 0, 2 * L - n)).astype(F32)

    gs = min(nt, max(1, FILTER_ROWS // half))
    grp_s = jnp.concatenate([grp] * gs, axis=0)

    def body(jg, carry):
        zs = []
        for s in range(gs):
            n1 = i * nt + jg * gs + s
            n = n1 + DFT_N1 * (r2 + jnp.where(grp, half, 0))
            ang1 = sign * (fvec * (n1.astype(F32) * scale)) + ph_ref[...]
            feats = jnp.cos(ang1) * cb_s[...] - jnp.sin(ang1) * sb_s[...]
            feats = jnp.where(n == L, jnp.cos(ph_ref[...]), feats)
            zs.append(jnp.where(feat == 0, lag(n) / (L - 1),
                                jnp.where(feat <= 2 * HYENA_BANDS, feats, 0.0)))
        z = jnp.concatenate(zs, axis=0)
        h = jnp.sin(fr * (_dot3(z, w1h_ref[...], w1l_ref[...]) + b1_ref[...]))
        h = jnp.sin(fr * (_dot3(h, w2h_ref[...], w2l_ref[...]) + b2_ref[...]))
        h = jnp.sin(fr * (_dot3(h, w3h_ref[...], w3l_ref[...]) + b3_ref[...]))
        taps = []
        for gi in range(2):
            cs = slice(gi * C, (gi + 1) * C)
            hg = jnp.where(grp_s == (gi == 1), h, 0.0)
            tp = _dot3(hg, wfh_ref[:, cs], wfl_ref[:, cs])
            dec = jnp.abs(dec_ref[gi:gi + 1, :])
            parts = []
            for s in range(gs):
                n_col = i * nt + jg * gs + s + DFT_N1 * (r_col + gi * half)
                tps = tp[s * half:(s + 1) * half] * jnp.exp(-(lag(n_col) / (L - 1)) * dec)
                l1_ref[...] += jnp.sum(jnp.abs(tps), axis=0, keepdims=True)
                parts.append(jnp.where(n_col == L, 0.0, tps).astype(BF16))
            taps.append(parts)
        for s in range(gs):
            out_ref[jg * gs + s] = _dot(m_ref[...], jnp.concatenate([taps[0][s], taps[1][s]], axis=0))
        return carry

    lax.fori_loop(0, nt // gs, body, 0)


def _blockdiag2(w):
    z = jnp.zeros_like(w)
    return jnp.concatenate([jnp.concatenate([w, z], axis=1), jnp.concatenate([z, w], axis=1)], axis=0)


def _filter_dft1(L, m1, w1, b1, w2, b2, w3, b3, fr, wfo, dec):
    C = wfo.shape[1] // 2
    hid = FILTER_HIDDEN
    n2 = m1.shape[1]
    fgrid = jnp.linspace(1e-4, HYENA_BANDS - 1, HYENA_BANDS, dtype=F32)
    fhalf = jnp.zeros((hid,), F32).at[1:1 + HYENA_BANDS].set(fgrid).at[1 + HYENA_BANDS:1 + 2 * HYENA_BANDS].set(fgrid)
    phalf = jnp.zeros((hid,), F32).at[1 + HYENA_BANDS:1 + 2 * HYENA_BANDS].set(0.5 * math.pi)
    two = lambda a: jnp.concatenate([a.reshape(1, -1), a.reshape(1, -1)], axis=1)
    w1p = jnp.zeros((hid, hid), F32).at[:w1.shape[0]].set(w1)
    mats = []
    for wmat in (_blockdiag2(w1p), _blockdiag2(w2), _blockdiag2(w3)):
        mats.append(_split(wmat))
    wst = jnp.concatenate([wfo, wfo], axis=0)
    wfh, wfl = _split(wst)
    args = (m1, two(fhalf), two(phalf), mats[0][0], mats[0][1], two(b1), mats[1][0], mats[1][1], two(b2),
            mats[2][0], mats[2][1], two(b3), two(fr), wfh, wfl, dec)
    full = lambda a: pl.BlockSpec(a.shape, lambda i: (0,) * a.ndim)
    nt = SUBLANES
    return pl.pallas_call(
        functools.partial(_filter_dft1_kernel, L),
        grid=(DFT_N1 // nt,), in_specs=[full(a) for a in args],
        out_specs=(pl.BlockSpec((nt, 2 * n2, C), lambda i: (i, 0, 0)), pl.BlockSpec((1, C), lambda i: (0, 0))),
        out_shape=(jax.ShapeDtypeStruct((DFT_N1, 2 * n2, C), F32), jax.ShapeDtypeStruct((1, C), F32)),
        scratch_shapes=[pltpu.VMEM((n2 // 2, 2 * hid), F32), pltpu.VMEM((n2 // 2, 2 * hid), F32)],
        compiler_params=_params("arbitrary"),
        name="filter_dft1")(*args)


@functools.lru_cache(maxsize=None)
def _dft_tables(n1, n2):
    n = n1 * n2
    h = n2 // 2
    a2 = 2.0 * np.pi * np.outer(np.arange(n2), np.arange(n2)) / n2
    c2, s2 = np.cos(a2), np.sin(a2)
    m1d = np.empty((n2, 2, n2))
    m1d[:, 0, :h], m1d[:, 0, h:] = c2[:, :h], s2[:, :h]
    m1d[:, 1, :h], m1d[:, 1, h:] = -s2[:, :h], c2[:, :h]
    m1f = np.stack([c2, -s2], axis=1)
    a1 = 2.0 * np.pi * np.outer(np.arange(n1), np.arange(n1)) / n1
    f1r, f1i = np.cos(a1), -np.sin(a1)
    pg = np.block([[f1r, -f1i], [f1i, f1r]])
    pgs = np.block([[-f1i, -f1r], [f1r, -f1i]])
    qd = np.block([[f1r, f1i], [-f1i, f1r]])
    at = 2.0 * np.pi * np.outer(np.arange(n2), np.arange(n1)) / n
    tr, ti = np.cos(at), -np.sin(at)
    txr = np.concatenate([tr, tr], axis=1)
    txi = np.concatenate([ti, ti], axis=1)
    er, ei = c2[:h] / n, s2[:h] / n
    pp = np.block([[er, -ei], [ei, er]])
    pps = np.block([[-ei, -er], [er, -ei]])
    ur, ui = np.cos(at).T, np.sin(at).T
    uxr = np.concatenate([ur, ur], axis=1)
    uxi = np.concatenate([ui, ui], axis=1)
    f32 = lambda a: np.asarray(a, np.float32)
    return dict(m1d=f32(m1d.reshape(2 * n2, n2)), m1f=f32(m1f.reshape(2 * n2, n2)),
                pg=f32(pg), pgs=f32(pgs), qd=f32(qd), txr=f32(txr), txi=f32(txi),
                pp=f32(pp), pps=f32(pps), uxr=f32(uxr), uxi=f32(uxi))


def _dft1_kernel(m_ref, x_ref, out_ref):
    nt = out_ref.shape[0]
    m = m_ref[...]
    for j in range(nt):
        parts = [x_ref[b, :, j, :] for b in range(x_ref.shape[0])]
        x = parts[0] if len(parts) == 1 else jnp.concatenate(parts, axis=0)
        out_ref[j] = _dot(m, x.astype(BF16))


def _dft1(m1, x):
    parts, rows, n1, C = x.shape
    nt = SUBLANES
    return pl.pallas_call(
        _dft1_kernel, grid=(n1 // nt,),
        in_specs=[pl.BlockSpec(m1.shape, lambda i: (0, 0)),
                  pl.BlockSpec((parts, rows, nt, C), lambda i: (0, 0, i, 0))],
        out_specs=pl.BlockSpec((nt, m1.shape[0], C), lambda i: (i, 0, 0)),
        out_shape=jax.ShapeDtypeStruct((n1, m1.shape[0], C), F32),
        compiler_params=_params("parallel"), name="dft_stage1")(m1, x)


def _conv_mid_kernel(a_ref, af_ref, l1_ref, pg_ref, pgs_ref, txr_ref, txi_ref, qd_ref, out_ref):
    n1 = a_ref.shape[0]
    qd = qd_ref[...]
    inv = 1.0 / l1_ref[...]
    for j in range(out_ref.shape[0]):
        gd = (pg_ref[...] * txr_ref[j:j + 1, :] + pgs_ref[...] * txi_ref[j:j + 1, :]).astype(BF16)
        rows = lambda ref: jnp.concatenate([ref[:, 2 * j, :], ref[:, 2 * j + 1, :]], axis=0).astype(BF16)
        x = _dot(gd, rows(a_ref))
        kf = _dot(gd, rows(af_ref)) * inv
        xr, xi = x[:n1], x[n1:]
        kr, ki = kf[:n1], kf[n1:]
        y = jnp.concatenate([xr * kr - xi * ki, xr * ki + xi * kr], axis=0).astype(BF16)
        out_ref[j] = _dot(qd, y).reshape(out_ref.shape[1:])


def _conv_mid(a, af, l1, tabs):
    n1, rows, C = a.shape
    n2 = rows // 2
    kt = SUBLANES
    ablk = pl.BlockSpec((n1, 2 * kt, C), lambda i: (0, i, 0))
    mat = pl.BlockSpec((2 * n1, 2 * n1), lambda i: (0, 0))
    tw = pl.BlockSpec((kt, 2 * n1), lambda i: (i, 0))
    return pl.pallas_call(
        _conv_mid_kernel, grid=(n2 // kt,),
        in_specs=[ablk, ablk, pl.BlockSpec((1, C), lambda i: (0, 0)), mat, mat, tw, tw, mat],
        out_specs=pl.BlockSpec((kt, 2, n1, C), lambda i: (i, 0, 0, 0)),
        out_shape=jax.ShapeDtypeStruct((n2, 2, n1, C), F32),
        compiler_params=_params("parallel"), name="conv_mid")(
            a, af, l1, tabs["pg"], tabs["pgs"], tabs["txr"], tabs["txi"], tabs["qd"])


def _conv_out_kernel(b_ref, pp_ref, pps_ref, uxr_ref, uxi_ref, out_ref):
    h = out_ref.shape[1]
    for j in range(out_ref.shape[2]):
        hm = (pp_ref[...] * uxr_ref[j:j + 1, :] + pps_ref[...] * uxi_ref[j:j + 1, :]).astype(BF16)
        bcat = jnp.concatenate([b_ref[:, 0, j, :], b_ref[:, 1, j, :]], axis=0).astype(BF16)
        y = _dot(hm, bcat)
        out_ref[0, :, j, :] = y[:h]
        out_ref[1, :, j, :] = y[h:]


def _conv_out(bm, tabs):
    n2, _, n1, C = bm.shape
    nt = SUBLANES
    mat = pl.BlockSpec((n2, 2 * n2), lambda i: (0, 0))
    tw = pl.BlockSpec((nt, 2 * n2), lambda i: (i, 0))
    return pl.pallas_call(
        _conv_out_kernel, grid=(n1 // nt,),
        in_specs=[pl.BlockSpec((n2, 2, nt, C), lambda i: (0, 0, i, 0)), mat, mat, tw, tw],
        out_specs=pl.BlockSpec((2, n2 // 2, nt, C), lambda i: (0, 0, i, 0)),
        out_shape=jax.ShapeDtypeStruct((2, n2 // 2, n1, C), F32),
        compiler_params=_params("parallel"), name="conv_out")(
            bm, tabs["pp"], tabs["pps"], tabs["uxr"], tabs["uxi"])


def _long_conv(zc, filt):
    B, L, C = zc.shape
    assert B == 2, "the two batch rows are packed as one complex sequence"
    n1 = DFT_N1
    n2 = 2 * L // n1
    tabs = {k: jnp.asarray(v) for k, v in _dft_tables(n1, n2).items()}
    af, l1 = _filter_dft1(L, tabs["m1f"].astype(BF16), *filt)
    a = _dft1(tabs["m1d"].astype(BF16), zc.reshape(B, n2 // 2, n1, C))
    bm = _conv_mid(a, af, l1, tabs)
    return _conv_out(bm, tabs).reshape(B, L, C)


def _merge_kernel(x_ref, hf_ref, hb_ref, o_ref, y_ref, zc_ref, x0_ref, hn_ref, skip_ref, g_ref,
                  wga_ref, bga_ref, wgb_ref, bgb_ref, wa_ref, wb_ref, wo_ref, out_ref):
    x = x_ref[0]
    n = _rms(x, g_ref[...]).astype(BF16)
    ga = jax.nn.sigmoid(_dot(n, wga_ref[...]) + bga_ref[...])
    gb = jax.nn.sigmoid(_dot(n, wgb_ref[...]) + bgb_ref[...])
    heads = []
    for h in range(MLSTM_HEADS):
        hs = slice(h * HEAD_DIM, (h + 1) * HEAD_DIM)
        s = hf_ref[0, :, hs] + hb_ref[0, :, hs]
        dv = s - jnp.mean(s, axis=1, keepdims=True)
        var = jnp.mean(dv * dv, axis=1, keepdims=True)
        heads.append((dv * lax.rsqrt(var + EPS) * hn_ref[:, hs]
                      * jax.nn.sigmoid(o_ref[0, :, hs])).astype(BF16))
    ml = jnp.concatenate(heads, axis=1)
    hy = (x0_ref[0] * (y_ref[0] + skip_ref[...] * zc_ref[0])).astype(BF16)
    merged = ga * _dot(ml, wa_ref[...]) + gb * _dot(hy, wb_ref[...])
    out_ref[0] = x + _dot(merged.astype(BF16), wo_ref[...])


def _merge(x, hf, hb, o, y, zc, x0, hn, skip, g, wga, bga, wgb, bgb, wa, wb, wo):
    B, L, D = x.shape
    tm = min(TOKEN_BLOCK, L)
    full = lambda w: pl.BlockSpec(w.shape, lambda bb, i: (0,) * w.ndim)
    tok = lambda w: pl.BlockSpec((1, tm, w), lambda bb, i: (bb, i, 0))
    ws = (hn, skip, g, wga, bga, wgb, bgb, wa, wb, wo)
    acts = (hf, hb, o, y, zc, x0)
    return pl.pallas_call(
        _merge_kernel, grid=(B, L // tm),
        in_specs=[tok(D)] + [tok(a.shape[2]) for a in acts] + [full(w) for w in ws],
        out_specs=tok(D), out_shape=jax.ShapeDtypeStruct(x.shape, F32),
        compiler_params=_params("parallel", "parallel"), name="merge")(x, *acts, *ws)


def _mlp_kernel(final, nchunk, x_ref, g_ref, w1_ref, w2_ref, gf_ref, out_ref):
    x = x_ref[0]
    n = _rms(x, g_ref[...]).astype(BF16)
    fc = w1_ref.shape[1] // nchunk
    acc = x
    for c in range(nchunk):
        hcl = jnp.maximum(_dot(n, w1_ref[:, c * fc:(c + 1) * fc]), 0.0)
        acc = acc + _dot((hcl * hcl).astype(BF16), w2_ref[c * fc:(c + 1) * fc, :])
    out_ref[0] = _rms(acc, gf_ref[...]) if final else acc


def _mlp(x, g, w1, w2, gf, final):
    B, L, D = x.shape
    tm = min(TOKEN_BLOCK, L)
    full = lambda w: pl.BlockSpec(w.shape, lambda bb, i: (0,) * w.ndim)
    tok = pl.BlockSpec((1, tm, D), lambda bb, i: (bb, i, 0))
    return pl.pallas_call(
        functools.partial(_mlp_kernel, final, 4), grid=(B, L // tm),
        in_specs=[tok, full(g), full(w1), full(w2), full(gf)],
        out_specs=tok, out_shape=jax.ShapeDtypeStruct(x.shape, F32),
        compiler_params=_params("parallel", "parallel"), name="mlp")(x, g, w1, w2, gf)


def _trunk(x, p):
    depth = p["w_in"].shape[0]
    D = x.shape[2]
    L = x.shape[1]
    dm = HEAD_DIM * MLSTM_HEADS
    ng = 4 * MLSTM_HEADS
    dh = p["hy_skip"].shape[1]
    o_g, o_u, o_ga = 4 * dm, 4 * dm + ng, 4 * dm + ng + 3 * dh
    row = lambda a: a.reshape(1, -1)
    col = lambda a: a.reshape(-1, 1)
    for l in range(depth):
        w_in = p["w_in"][l]
        b_in = p["b_in"][l]
        w_qvo = jnp.concatenate([w_in[:, :dm], w_in[:, 2 * dm:4 * dm]], axis=1)
        b_qvo = jnp.concatenate([b_in[:dm], b_in[2 * dm:4 * dm]])
        w_g = w_in[:, o_g:o_u]
        q, kt, v, o, g_col, g_row, x0, zc = _inproj(
            x, row(p["norm_mix"][l]), w_qvo.astype(BF16), row(b_qvo),
            w_in[:, dm:2 * dm].T.astype(BF16), col(b_in[dm:2 * dm]),
            w_g.T.astype(BF16), col(b_in[o_g:o_u]),
            w_in[:, o_u:o_ga].astype(BF16), row(b_in[o_u:o_ga]),
            p["hy_conv_w"][l], row(p["hy_conv_b"][l]))
        h_fwd, h_bwd = _mlstm(q, kt, v, g_col, g_row)

        y = _long_conv(zc, (p["hy_w1"][l], p["hy_b1"][l], p["hy_w2"][l], p["hy_b2"][l], p["hy_w3"][l],
                            p["hy_b3"][l], p["hy_freq"][l], p["hy_w_fo"][l], p["hy_decay"][l]))

        x = _merge(x, h_fwd, h_bwd, o, y, zc, x0, row(p["mlstm_norm"][l]), row(p["hy_skip"][l]),
                   row(p["norm_mix"][l]),
                   w_in[:, o_ga:o_ga + D].astype(BF16), row(b_in[o_ga:o_ga + D]),
                   w_in[:, o_ga + D:].astype(BF16), row(b_in[o_ga + D:]),
                   p["w_branch_a"][l].astype(BF16), p["w_branch_b"][l].astype(BF16),
                   p["w_out"][l].astype(BF16))
        x = _mlp(x, row(p["norm_mlp"][l]), p["w_mlp1"][l].astype(BF16), p["w_mlp2"][l].astype(BF16),
                 row(p["norm_final"]), l == depth - 1)
    return x


def kernel(x_prompt, x_sample, norm_mix, w_in, b_in, mlstm_norm, hy_conv_w, hy_conv_b, hy_w1, hy_b1, hy_w2, hy_b2, hy_w3, hy_b3, hy_freq, hy_w_fo, hy_decay, hy_skip, w_branch_a, w_branch_b, w_out, norm_mlp, w_mlp1, w_mlp2, norm_final):
    p = dict(norm_mix=norm_mix, w_in=w_in, b_in=b_in, mlstm_norm=mlstm_norm, hy_conv_w=hy_conv_w,
             hy_conv_b=hy_conv_b, hy_w1=hy_w1, hy_b1=hy_b1, hy_w2=hy_w2, hy_b2=hy_b2, hy_w3=hy_w3,
             hy_b3=hy_b3, hy_freq=hy_freq, hy_w_fo=hy_w_fo, hy_decay=hy_decay, hy_skip=hy_skip,
             w_branch_a=w_branch_a, w_branch_b=w_branch_b, w_out=w_out, norm_mlp=norm_mlp,
             w_mlp1=w_mlp1, w_mlp2=w_mlp2, norm_final=norm_final)
    return (_trunk(x_prompt, p), _trunk(x_sample, p))
```

```python
import functools
import math

import numpy as np
import jax
import jax.numpy as jnp
from jax import lax
from jax.experimental import pallas as pl
from jax.experimental.pallas import tpu as pltpu

F32 = jnp.float32
BF16 = jnp.bfloat16

EPS = 1e-6
MLSTM_HEADS = 4
HEAD_DIM = 128
CHUNK = 256
HYENA_BANDS = 16
FILTER_HIDDEN = 64
DFT_N1 = 128
FILTER_ROWS = 512
SUBLANES = 8
TOKEN_BLOCK = 512
VMEM_LIMIT = 48 * 1024 * 1024
HIGHEST = lax.Precision.HIGHEST


def _params(*sem):
    return pltpu.CompilerParams(dimension_semantics=sem, vmem_limit_bytes=VMEM_LIMIT)


def _rms(x, g):
    return x * lax.rsqrt(jnp.mean(x * x, axis=-1, keepdims=True) + EPS) * g


def _dot(a, b):
    return jnp.dot(a, b, preferred_element_type=F32)


def _dot_nt(a, b):
    return lax.dot_general(a, b, (((1,), (1,)), ((), ())), preferred_element_type=F32)


def _split(a):
    hi = a.astype(BF16)
    return hi, (a - hi.astype(F32)).astype(BF16)


def _dot3(a, b_hi, b_lo):
    a_hi, a_lo = _split(a)
    return _dot(a_hi, b_hi) + (_dot(a_hi, b_lo) + _dot(a_lo, b_hi))


def _inproj_kernel(x_ref, xp_ref, xn_ref, g_ref, wm_ref, bm_ref, wkt_ref, bk_ref,
                   wgt_ref, bgt_ref, wu_ref, bu_ref, cw_ref, cb_ref,
                   q_ref, kt_ref, v_ref, o_ref, gc_ref, gr_ref, x0_ref, zc_ref):
    i = pl.program_id(1)
    last = pl.num_programs(1) - 1
    g = g_ref[...]
    n = _rms(x_ref[0], g).astype(BF16)
    dm = HEAD_DIM * MLSTM_HEADS
    pm = _dot(n, wm_ref[...]) + bm_ref[...]
    q_ref[0] = (pm[:, :dm] * (HEAD_DIM ** -0.5)).astype(BF16)
    v_ref[0] = pm[:, dm:2 * dm].astype(BF16)
    o_ref[0] = pm[:, 2 * dm:]
    kt_ref[0] = (_dot_nt(wkt_ref[...], n) + bk_ref[...]).astype(BF16)
    gates = _dot_nt(wgt_ref[...], n) + bgt_ref[...]
    gr_ref[0] = gates
    gc_ref[0] = gates.T

    tm = n.shape[0]
    halo = _rms(jnp.concatenate([xp_ref[0], xn_ref[0]], axis=0), g).astype(BF16)
    u_all = _dot(jnp.concatenate([n, halo], axis=0), wu_ref[...]) + bu_ref[...]
    u = u_all[:tm]
    u_prev = jnp.where(i > 0, u_all[tm + SUBLANES - 1:tm + SUBLANES], 0.0)
    u_next = jnp.where(i < last, u_all[tm + SUBLANES:tm + SUBLANES + 1], 0.0)
    row = lax.broadcasted_iota(jnp.int32, (tm, 1), 0)
    u_m1 = jnp.where(row == 0, u_prev, pltpu.roll(u, 1, 0))
    u_p1 = jnp.where(row == tm - 1, u_next, pltpu.roll(u, tm - 1, 0))
    cw = cw_ref[...]
    c = u_m1 * cw[0:1] + u * cw[1:2] + u_p1 * cw[2:3] + cb_ref[...]
    dh = c.shape[1] // 3
    x0_ref[0] = c[:, :dh]
    zc_ref[0] = c[:, dh:2 * dh] * c[:, 2 * dh:]


def _inproj(x, g, wm, bm, wkt, bk, wgt, bgt, wu, bu, cw, cb):
    B, L, D = x.shape
    tm = min(TOKEN_BLOCK, L)
    nb = L // tm
    r8 = tm // SUBLANES
    dm = wkt.shape[0]
    dh = wu.shape[1] // 3
    ng = wgt.shape[0]
    ws = (g, wm, bm, wkt, bk, wgt, bgt, wu, bu, cw, cb)
    full = lambda a: pl.BlockSpec(a.shape, lambda b, i: (0,) * a.ndim)
    tok = lambda w: pl.BlockSpec((1, tm, w), lambda b, i: (b, i, 0))
    tr = lambda w: pl.BlockSpec((1, w, tm), lambda b, i: (b, 0, i))
    in_specs = [
        tok(D),
        pl.BlockSpec((1, SUBLANES, D), lambda b, i: (b, jnp.maximum(i * r8 - 1, 0), 0)),
        pl.BlockSpec((1, SUBLANES, D), lambda b, i: (b, jnp.minimum((i + 1) * r8, L // SUBLANES - 1), 0)),
    ] + [full(w) for w in ws]
    sds = jax.ShapeDtypeStruct
    out_shape = (sds((B, L, dm), BF16), sds((B, dm, L), BF16), sds((B, L, dm), BF16), sds((B, L, dm), F32),
                 sds((B, L, ng), F32), sds((B, ng, L), F32), sds((B, L, dh), F32), sds((B, L, dh), F32))
    out_specs = (tok(dm), tr(dm), tok(dm), tok(dm), tok(ng), tr(ng), tok(dh), tok(dh))
    return pl.pallas_call(
        _inproj_kernel, grid=(B, nb), in_specs=in_specs, out_specs=out_specs,
        out_shape=out_shape, compiler_params=_params("parallel", "parallel"),
        name="inproj")(x, x, x, *ws)


def _log_sigmoid(x):
    return jnp.minimum(x, 0.0) - jnp.log1p(jnp.exp(-jnp.abs(x)))


def _mlstm_dir(reverse, q_ref, kt_ref, v_ref, gc_ref, gr_ref, out_ref, c_s, m_s, slot):
    nh = MLSTM_HEADS
    nb = q_ref.shape[0]
    nq = nb * nh
    ch = q_ref.shape[1]
    off = 2 * nh if reverse else 0
    ri = lax.broadcasted_iota(jnp.int32, (ch, ch), 0)
    ci = lax.broadcasted_iota(jnp.int32, (ch, ch), 1)
    mask = (ri <= ci) if reverse else (ci <= ri)
    tri_col = mask.astype(F32)
    tri_row = ((ci <= ri) if reverse else (ri <= ci)).astype(F32)

    li_col = jnp.concatenate([gc_ref[b][:, off:off + nh] for b in range(nb)], axis=1)
    lf_col = _log_sigmoid(jnp.concatenate([gc_ref[b][:, off + nh:off + 2 * nh] for b in range(nb)], axis=1))
    li_row = jnp.concatenate([gr_ref[b][off:off + nh, :] for b in range(nb)], axis=0)
    lf_row = _log_sigmoid(jnp.concatenate([gr_ref[b][off + nh:off + 2 * nh, :] for b in range(nb)], axis=0))
    b_col = jnp.dot(tri_col, lf_col, precision=HIGHEST, preferred_element_type=F32)
    b_row = jnp.dot(lf_row, tri_row, precision=HIGHEST, preferred_element_type=F32)
    beta_col = li_col - b_col
    beta_row = li_row - b_row
    cmx = beta_col
    rowi = lax.broadcasted_iota(jnp.int32, (ch, nq), 0)
    k = 1
    while k < ch:
        if reverse:
            cmx = jnp.maximum(cmx, jnp.where(rowi < ch - k, pltpu.roll(cmx, ch - k, 0), -jnp.inf))
        else:
            cmx = jnp.maximum(cmx, jnp.where(rowi >= k, pltpu.roll(cmx, k, 0), -jnp.inf))
        k *= 2
    last = 0 if reverse else ch - 1
    g_row = b_col[last:last + 1, :]
    m_loc = g_row + cmx[last:last + 1, :]
    m_prev = m_s[slot:slot + 1, 0:nq]
    mx = jnp.maximum(m_prev, cmx)
    s_inter = jnp.exp(m_prev - mx)
    inv_floor = jnp.exp(-mx - b_col)
    m_new = jnp.maximum(g_row + m_prev, m_loc)
    s_old = jnp.exp(g_row + m_prev - m_new)
    s_new = jnp.exp(m_loc - m_new)
    w_row = jnp.exp(beta_row - jnp.max(beta_row, axis=1, keepdims=True))
    m_s[slot:slot + 1, 0:nq] = m_new

    ones_col = (lax.broadcasted_iota(jnp.int32, (ch, HEAD_DIM), 1) == 0).astype(BF16)
    for b in range(nb):
        for h in range(nh):
            i = b * nh + h
            hs = slice(h * HEAD_DIM, (h + 1) * HEAD_DIM)
            q = q_ref[b, :, hs]
            kt = kt_ref[b, hs, :]
            v_aug = jnp.concatenate([v_ref[b, :, hs], ones_col], axis=1)
            d_mat = jnp.exp(jnp.where(mask, beta_row[i:i + 1, :] - mx[:, i:i + 1], -jnp.inf))
            scores = (_dot(q, kt) * d_mat).astype(BF16)
            c_prev = c_s[slot * nq + i]
            num = _dot(scores, v_aug) + s_inter[:, i:i + 1] * _dot(q, c_prev.astype(BF16))
            den = num[:, HEAD_DIM:HEAD_DIM + 1]
            out_ref[b, :, hs] = num[:, :HEAD_DIM] * (1.0 / jnp.maximum(jnp.abs(den), inv_floor[:, i:i + 1]))
            ktw = (kt.astype(F32) * w_row[i:i + 1, :]).astype(BF16)
            c_s[slot * nq + i] = s_old[:, i:i + 1] * c_prev + s_new[:, i:i + 1] * _dot(ktw, v_aug)


def _mlstm_kernel(qf, ktf, vf, gcf, grf, qb, ktb, vb, gcb, grb, of, ob, c_s, m_s):
    @pl.when(pl.program_id(0) == 0)
    def _():
        c_s[...] = jnp.zeros_like(c_s)
        m_s[...] = jnp.zeros_like(m_s)

    _mlstm_dir(False, qf, ktf, vf, gcf, grf, of, c_s, m_s, 0)
    _mlstm_dir(True, qb, ktb, vb, gcb, grb, ob, c_s, m_s, 1)


def _mlstm(q, kt, v, g_col, g_row):
    B, L, dm = q.shape
    ch = min(CHUNK, L)
    nc = L // ch
    specs = []
    for cidx in ((lambda c: c), (lambda c: nc - 1 - c)):
        tok = lambda w, cidx=cidx: pl.BlockSpec((B, ch, w), lambda c: (0, cidx(c), 0))
        tr = lambda w, cidx=cidx: pl.BlockSpec((B, w, ch), lambda c: (0, 0, cidx(c)))
        specs.append(([tok(dm), tr(dm), tok(dm), tok(g_col.shape[2]), tr(g_row.shape[1])], tok(dm)))
    args = [q, kt, v, g_col, g_row]
    out = jax.ShapeDtypeStruct((B, L, dm), F32)
    return pl.pallas_call(
        _mlstm_kernel, grid=(nc,), in_specs=specs[0][0] + specs[1][0],
        out_specs=(specs[0][1], specs[1][1]), out_shape=(out, out),
        scratch_shapes=[pltpu.VMEM((2 * B * MLSTM_HEADS, HEAD_DIM, 2 * HEAD_DIM), F32),
                        pltpu.VMEM((SUBLANES, 128), F32)],
        compiler_params=_params("arbitrary"), name="mlstm")(*args, *args)


N_FILTER_REFS = 16


def _filter_init(seq_len, refs, l1_ref, cb_s, sb_s):
    half = refs[0].shape[1] // 2
    hid = FILTER_HIDDEN
    grp = lax.broadcasted_iota(jnp.int32, (half, 2 * hid), 1) >= hid
    r2 = lax.broadcasted_iota(jnp.int32, (half, 2 * hid), 0)
    l1_ref[...] = jnp.zeros_like(l1_ref)
    lag0 = DFT_N1 * jnp.where(grp, half - r2, r2)
    ang = refs[1][...] * (lag0.astype(F32) * ((2.0 * math.pi) / seq_len))
    cb_s[...] = jnp.cos(ang)
    sb_s[...] = jnp.sin(ang)


def _filter_pass(seq_len, refs, n1_first, gs, store, l1_ref, cb_s, sb_s):
    (m_ref, fvec_ref, ph_ref, w1h_ref, w1l_ref, b1_ref, w2h_ref, w2l_ref, b2_ref, w3h_ref, w3l_ref,
     b3_ref, fr_ref, wfh_ref, wfl_ref, dec_ref) = refs
    half = m_ref.shape[1] // 2
    hid = FILTER_HIDDEN
    C = dec_ref.shape[1]
    L = seq_len
    lane = lax.broadcasted_iota(jnp.int32, (half, 2 * hid), 1)
    grp = lane >= hid
    feat = lane & (hid - 1)
    r2 = lax.broadcasted_iota(jnp.int32, (half, 2 * hid), 0)
    r_col = lax.broadcasted_iota(jnp.int32, (half, 1), 0)
    fvec = fvec_ref[...]
    fr = fr_ref[...]
    sign = jnp.where(grp[0:1], -1.0, 1.0)
    scale = (2.0 * math.pi) / L

    def lag(n):
        return jnp.where(n < L, n, jnp.where(n == L, 0, 2 * L - n)).astype(F32)

    grp_s = jnp.concatenate([grp] * gs, axis=0)
    zs = []
    for s in range(gs):
        n1 = n1_first + s
        n = n1 + DFT_N1 * (r2 + jnp.where(grp, half, 0))
        ang1 = sign * (fvec * (n1.astype(F32) * scale)) + ph_ref[...]
        feats = jnp.cos(ang1) * cb_s[...] - jnp.sin(ang1) * sb_s[...]
        feats = jnp.where(n == L, jnp.cos(ph_ref[...]), feats)
        zs.append(jnp.where(feat == 0, lag(n) / (L - 1),
                            jnp.where(feat <= 2 * HYENA_BANDS, feats, 0.0)))
    z = jnp.concatenate(zs, axis=0)
    h = jnp.sin(fr * (_dot3(z, w1h_ref[...], w1l_ref[...]) + b1_ref[...]))
    yield
    h = jnp.sin(fr * (_dot3(h, w2h_ref[...], w2l_ref[...]) + b2_ref[...]))
    yield
    h = jnp.sin(fr * (_dot3(h, w3h_ref[...], w3l_ref[...]) + b3_ref[...]))
    yield
    taps = []
    for gi in range(2):
        cs = slice(gi * C, (gi + 1) * C)
        hg = jnp.where(grp_s == (gi == 1), h, 0.0)
        tp = _dot3(hg, wfh_ref[:, cs], wfl_ref[:, cs])
        dec = jnp.abs(dec_ref[gi:gi + 1, :])
        parts = []
        for s in range(gs):
            n_col = n1_first + s + DFT_N1 * (r_col + gi * half)
            tps = tp[s * half:(s + 1) * half] * jnp.exp(-(lag(n_col) / (L - 1)) * dec)
            l1_ref[...] += jnp.sum(jnp.abs(tps), axis=0, keepdims=True)
            parts.append(jnp.where(n_col == L, 0.0, tps).astype(BF16))
        taps.append(parts)
    for s in range(gs):
        store(s, _dot(m_ref[...], jnp.concatenate([taps[0][s], taps[1][s]], axis=0)))


def _filter_dft1_kernel(seq_len, *refs):
    out_ref, l1_ref, cb_s, sb_s = refs[N_FILTER_REFS:]
    i = pl.program_id(0)
    nt = out_ref.shape[0]
    half = refs[0].shape[1] // 2
    gs = min(nt, max(1, FILTER_ROWS // half))

    @pl.when(i == 0)
    def _():
        _filter_init(seq_len, refs, l1_ref, cb_s, sb_s)

    def body(jg, carry):
        def store(s, val):
            out_ref[jg * gs + s] = val
        for _ in _filter_pass(seq_len, refs[:N_FILTER_REFS], i * nt + jg * gs, gs, store, l1_ref, cb_s, sb_s):
            pass
        return carry

    lax.fori_loop(0, nt // gs, body, 0)


def _blockdiag2(w):
    z = jnp.zeros_like(w)
    return jnp.concatenate([jnp.concatenate([w, z], axis=1), jnp.concatenate([z, w], axis=1)], axis=0)


def _filter_args(m1, w1, b1, w2, b2, w3, b3, fr, wfo, dec):
    hid = FILTER_HIDDEN
    fgrid = jnp.linspace(1e-4, HYENA_BANDS - 1, HYENA_BANDS, dtype=F32)
    fhalf = jnp.zeros((hid,), F32).at[1:1 + HYENA_BANDS].set(fgrid).at[1 + HYENA_BANDS:1 + 2 * HYENA_BANDS].set(fgrid)
    phalf = jnp.zeros((hid,), F32).at[1 + HYENA_BANDS:1 + 2 * HYENA_BANDS].set(0.5 * math.pi)
    two = lambda a: jnp.concatenate([a.reshape(1, -1), a.reshape(1, -1)], axis=1)
    w1p = jnp.zeros((hid, hid), F32).at[:w1.shape[0]].set(w1)
    mats = []
    for wmat in (_blockdiag2(w1p), _blockdiag2(w2), _blockdiag2(w3)):
        mats.append(_split(wmat))
    wst = jnp.concatenate([wfo, wfo], axis=0)
    wfh, wfl = _split(wst)
    args = (m1, two(fhalf), two(phalf), mats[0][0], mats[0][1], two(b1), mats[1][0], mats[1][1], two(b2),
            mats[2][0], mats[2][1], two(b3), two(fr), wfh, wfl, dec)
    assert len(args) == N_FILTER_REFS
    return args


def _filter_out(n2, C):
    shapes = (jax.ShapeDtypeStruct((DFT_N1, 2 * n2, C), F32), jax.ShapeDtypeStruct((1, C), F32))
    scratch = [pltpu.VMEM((n2 // 2, 2 * FILTER_HIDDEN), F32), pltpu.VMEM((n2 // 2, 2 * FILTER_HIDDEN), F32)]
    return shapes, scratch


def _filter_dft1(L, args):
    n2 = args[0].shape[1]
    C = args[-1].shape[1]
    full = lambda a: pl.BlockSpec(a.shape, lambda i: (0,) * a.ndim)
    nt = SUBLANES
    out_shape, scratch = _filter_out(n2, C)
    return pl.pallas_call(
        functools.partial(_filter_dft1_kernel, L),
        grid=(DFT_N1 // nt,), in_specs=[full(a) for a in args],
        out_specs=(pl.BlockSpec((nt, 2 * n2, C), lambda i: (i, 0, 0)), pl.BlockSpec((1, C), lambda i: (0, 0))),
        out_shape=out_shape, scratch_shapes=scratch,
        compiler_params=_params("arbitrary"),
        name="filter_dft1")(*args)


@functools.lru_cache(maxsize=None)
def _dft_tables(n1, n2):
    n = n1 * n2
    h = n2 // 2
    a2 = 2.0 * np.pi * np.outer(np.arange(n2), np.arange(n2)) / n2
    c2, s2 = np.cos(a2), np.sin(a2)
    m1d = np.empty((n2, 2, n2))
    m1d[:, 0, :h], m1d[:, 0, h:] = c2[:, :h], s2[:, :h]
    m1d[:, 1, :h], m1d[:, 1, h:] = -s2[:, :h], c2[:, :h]
    m1f = np.stack([c2, -s2], axis=1)
    a1 = 2.0 * np.pi * np.outer(np.arange(n1), np.arange(n1)) / n1
    f1r, f1i = np.cos(a1), -np.sin(a1)
    pg = np.block([[f1r, -f1i], [f1i, f1r]])
    pgs = np.block([[-f1i, -f1r], [f1r, -f1i]])
    qd = np.block([[f1r, f1i], [-f1i, f1r]])
    at = 2.0 * np.pi * np.outer(np.arange(n2), np.arange(n1)) / n
    tr, ti = np.cos(at), -np.sin(at)
    txr = np.concatenate([tr, tr], axis=1)
    txi = np.concatenate([ti, ti], axis=1)
    er, ei = c2[:h] / n, s2[:h] / n
    pp = np.block([[er, -ei], [ei, er]])
    pps = np.block([[-ei, -er], [er, -ei]])
    ur, ui = np.cos(at).T, np.sin(at).T
    uxr = np.concatenate([ur, ur], axis=1)
    uxi = np.concatenate([ui, ui], axis=1)
    f32 = lambda a: np.asarray(a, np.float32)
    return dict(m1d=f32(m1d.reshape(2 * n2, n2)), m1f=f32(m1f.reshape(2 * n2, n2)),
                pg=f32(pg), pgs=f32(pgs), qd=f32(qd), txr=f32(txr), txi=f32(txi),
                pp=f32(pp), pps=f32(pps), uxr=f32(uxr), uxi=f32(uxi))


def _dft1_kernel(m_ref, x_ref, out_ref):
    nt = out_ref.shape[0]
    m = m_ref[...]
    for j in range(nt):
        parts = [x_ref[b, :, j, :] for b in range(x_ref.shape[0])]
        x = parts[0] if len(parts) == 1 else jnp.concatenate(parts, axis=0)
        out_ref[j] = _dot(m, x.astype(BF16))


def _dft1(m1, x):
    parts, rows, n1, C = x.shape
    nt = SUBLANES
    return pl.pallas_call(
        _dft1_kernel, grid=(n1 // nt,),
        in_specs=[pl.BlockSpec(m1.shape, lambda i: (0, 0)),
                  pl.BlockSpec((parts, rows, nt, C), lambda i: (0, 0, i, 0))],
        out_specs=pl.BlockSpec((nt, m1.shape[0], C), lambda i: (i, 0, 0)),
        out_shape=jax.ShapeDtypeStruct((n1, m1.shape[0], C), F32),
        compiler_params=_params("parallel"), name="dft_stage1")(m1, x)


def _conv_mid_kernel(a_ref, af_ref, l1_ref, pg_ref, pgs_ref, txr_ref, txi_ref, qd_ref, out_ref):
    n1 = a_ref.shape[0]
    qd = qd_ref[...]
    inv = 1.0 / l1_ref[...]
    for j in range(out_ref.shape[0]):
        gd = (pg_ref[...] * txr_ref[j:j + 1, :] + pgs_ref[...] * txi_ref[j:j + 1, :]).astype(BF16)
        rows = lambda ref: jnp.concatenate([ref[:, 2 * j, :], ref[:, 2 * j + 1, :]], axis=0).astype(BF16)
        x = _dot(gd, rows(a_ref))
        kf = _dot(gd, rows(af_ref)) * inv
        xr, xi = x[:n1], x[n1:]
        kr, ki = kf[:n1], kf[n1:]
        y = jnp.concatenate([xr * kr - xi * ki, xr * ki + xi * kr], axis=0).astype(BF16)
        out_ref[j] = _dot(qd, y).reshape(out_ref.shape[1:])


def _conv_mid(a, af, l1, tabs):
    n1, rows, C = a.shape
    n2 = rows // 2
    kt = SUBLANES
    ablk = pl.BlockSpec((n1, 2 * kt, C), lambda i: (0, i, 0))
    mat = pl.BlockSpec((2 * n1, 2 * n1), lambda i: (0, 0))
    tw = pl.BlockSpec((kt, 2 * n1), lambda i: (i, 0))
    return pl.pallas_call(
        _conv_mid_kernel, grid=(n2 // kt,),
        in_specs=[ablk, ablk, pl.BlockSpec((1, C), lambda i: (0, 0)), mat, mat, tw, tw, mat],
        out_specs=pl.BlockSpec((kt, 2, n1, C), lambda i: (i, 0, 0, 0)),
        out_shape=jax.ShapeDtypeStruct((n2, 2, n1, C), F32),
        compiler_params=_params("parallel"), name="conv_mid")(
            a, af, l1, tabs["pg"], tabs["pgs"], tabs["txr"], tabs["txi"], tabs["qd"])


def _conv_out_kernel(b_ref, pp_ref, pps_ref, uxr_ref, uxi_ref, out_ref):
    h = out_ref.shape[1]
    for j in range(out_ref.shape[2]):
        hm = (pp_ref[...] * uxr_ref[j:j + 1, :] + pps_ref[...] * uxi_ref[j:j + 1, :]).astype(BF16)
        bcat = jnp.concatenate([b_ref[:, 0, j, :], b_ref[:, 1, j, :]], axis=0).astype(BF16)
        y = _dot(hm, bcat)
        out_ref[0, :, j, :] = y[:h]
        out_ref[1, :, j, :] = y[h:]


def _conv_out(bm, tabs):
    n2, _, n1, C = bm.shape
    nt = SUBLANES
    mat = pl.BlockSpec((n2, 2 * n2), lambda i: (0, 0))
    tw = pl.BlockSpec((nt, 2 * n2), lambda i: (i, 0))
    return pl.pallas_call(
        _conv_out_kernel, grid=(n1 // nt,),
        in_specs=[pl.BlockSpec((n2, 2, nt, C), lambda i: (0, 0, i, 0)), mat, mat, tw, tw],
        out_specs=pl.BlockSpec((2, n2 // 2, nt, C), lambda i: (0, 0, i, 0)),
        out_shape=jax.ShapeDtypeStruct((2, n2 // 2, n1, C), F32),
        compiler_params=_params("parallel"), name="conv_out")(
            bm, tabs["pp"], tabs["pps"], tabs["uxr"], tabs["uxi"])


def _long_conv(zc, af, l1):
    B, L, C = zc.shape
    assert B == 2, "the two batch rows are packed as one complex sequence"
    n1 = DFT_N1
    n2 = 2 * L // n1
    tabs = {k: jnp.asarray(v) for k, v in _dft_tables(n1, n2).items()}
    a = _dft1(tabs["m1d"].astype(BF16), zc.reshape(B, n2 // 2, n1, C))
    bm = _conv_mid(a, af, l1, tabs)
    return _conv_out(bm, tabs).reshape(B, L, C)


def _merge_kernel(x_ref, hf_ref, hb_ref, o_ref, y_ref, zc_ref, x0_ref, hn_ref, skip_ref, g_ref,
                  wga_ref, bga_ref, wgb_ref, bgb_ref, wa_ref, wb_ref, wo_ref, out_ref):
    x = x_ref[0]
    n = _rms(x, g_ref[...]).astype(BF16)
    ga = jax.nn.sigmoid(_dot(n, wga_ref[...]) + bga_ref[...])
    gb = jax.nn.sigmoid(_dot(n, wgb_ref[...]) + bgb_ref[...])
    heads = []
    for h in range(MLSTM_HEADS):
        hs = slice(h * HEAD_DIM, (h + 1) * HEAD_DIM)
        s = hf_ref[0, :, hs] + hb_ref[0, :, hs]
        dv = s - jnp.mean(s, axis=1, keepdims=True)
        var = jnp.mean(dv * dv, axis=1, keepdims=True)
        heads.append((dv * lax.rsqrt(var + EPS) * hn_ref[:, hs]
                      * jax.nn.sigmoid(o_ref[0, :, hs])).astype(BF16))
    ml = jnp.concatenate(heads, axis=1)
    hy = (x0_ref[0] * (y_ref[0] + skip_ref[...] * zc_ref[0])).astype(BF16)
    merged = ga * _dot(ml, wa_ref[...]) + gb * _dot(hy, wb_ref[...])
    out_ref[0] = x + _dot(merged.astype(BF16), wo_ref[...])


def _merge(x, hf, hb, o, y, zc, x0, hn, skip, g, wga, bga, wgb, bgb, wa, wb, wo):
    B, L, D = x.shape
    tm = min(TOKEN_BLOCK, L)
    full = lambda w: pl.BlockSpec(w.shape, lambda bb, i: (0,) * w.ndim)
    tok = lambda w: pl.BlockSpec((1, tm, w), lambda bb, i: (bb, i, 0))
    ws = (hn, skip, g, wga, bga, wgb, bgb, wa, wb, wo)
    acts = (hf, hb, o, y, zc, x0)
    return pl.pallas_call(
        _merge_kernel, grid=(B, L // tm),
        in_specs=[tok(D)] + [tok(a.shape[2]) for a in acts] + [full(w) for w in ws],
        out_specs=tok(D), out_shape=jax.ShapeDtypeStruct(x.shape, F32),
        compiler_params=_params("parallel", "parallel"), name="merge")(x, *acts, *ws)


def _mlp_kernel(final, nchunk, host_len, x_ref, g_ref, w1_ref, w2_ref, gf_ref, *rest):
    out_ref = rest[N_FILTER_REFS] if host_len else rest[0]
    if host_len:
        af_ref, l1_ref, cb_s, sb_s = rest[N_FILTER_REFS + 1:]
        step = pl.program_id(0) * pl.num_programs(1) + pl.program_id(1)

        @pl.when(step == 0)
        def _():
            _filter_init(host_len, rest, l1_ref, cb_s, sb_s)

        gs = af_ref.shape[0]

        def store(s, val):
            af_ref[s] = val
        hosted = _filter_pass(host_len, rest[:N_FILTER_REFS], step * gs, gs, store, l1_ref, cb_s, sb_s)
    else:
        hosted = iter(())

    x = x_ref[0]
    n = _rms(x, g_ref[...]).astype(BF16)
    fc = w1_ref.shape[1] // nchunk
    acc = x
    for c in range(nchunk):
        next(hosted, None)
        hcl = jnp.maximum(_dot(n, w1_ref[:, c * fc:(c + 1) * fc]), 0.0)
        acc = acc + _dot((hcl * hcl).astype(BF16), w2_ref[c * fc:(c + 1) * fc, :])
    for _ in hosted:
        pass
    out_ref[0] = _rms(acc, gf_ref[...]) if final else acc


def _mlp(x, g, w1, w2, gf, final, host=None):
    B, L, D = x.shape
    tm = min(TOKEN_BLOCK, L)
    steps = B * (L // tm)
    full = lambda w: pl.BlockSpec(w.shape, lambda bb, i: (0,) * w.ndim)
    once = lambda w: pl.BlockSpec(w.shape, lambda bb, i: (0,) * w.ndim, pipeline_mode=pl.Buffered(1))
    tok = pl.BlockSpec((1, tm, D), lambda bb, i: (bb, i, 0))
    args = [x, g, w1, w2, gf]
    in_specs = [tok, full(g), once(w1), once(w2), full(gf)]
    out_specs, out_shape, scratch = tok, jax.ShapeDtypeStruct(x.shape, F32), []
    sem = ("parallel", "parallel")
    host_len = 0
    if host is not None:
        host_len, fargs = host
        n2 = fargs[0].shape[1]
        C = fargs[-1].shape[1]
        gs = DFT_N1 // steps
        assert gs * steps == DFT_N1
        args += list(fargs)
        in_specs += [full(a) for a in fargs]
        per_l = L // tm
        fshape, scratch = _filter_out(n2, C)
        out_specs = (tok, pl.BlockSpec((gs, 2 * n2, C), lambda bb, i: (bb * per_l + i, 0, 0)),
                     pl.BlockSpec((1, C), lambda bb, i: (0, 0)))
        out_shape = (out_shape,) + fshape
        sem = ("arbitrary", "arbitrary")
    return pl.pallas_call(
        functools.partial(_mlp_kernel, final, 4, host_len), grid=(B, L // tm),
        in_specs=in_specs, out_specs=out_specs, out_shape=out_shape, scratch_shapes=scratch,
        compiler_params=_params(*sem), name="mlp")(*args)


def _layer(x, p, l, filt, host):
    depth = p["w_in"].shape[0]
    D = x.shape[2]
    dm = HEAD_DIM * MLSTM_HEADS
    ng = 4 * MLSTM_HEADS
    dh = p["hy_skip"].shape[1]
    o_g, o_u, o_ga = 4 * dm, 4 * dm + ng, 4 * dm + ng + 3 * dh
    row = lambda a: a.reshape(1, -1)
    col = lambda a: a.reshape(-1, 1)
    w_in = p["w_in"][l]
    b_in = p["b_in"][l]
    w_qvo = jnp.concatenate([w_in[:, :dm], w_in[:, 2 * dm:4 * dm]], axis=1)
    b_qvo = jnp.concatenate([b_in[:dm], b_in[2 * dm:4 * dm]])
    q, kt, v, o, g_col, g_row, x0, zc = _inproj(
        x, row(p["norm_mix"][l]), w_qvo.astype(BF16), row(b_qvo),
        w_in[:, dm:2 * dm].T.astype(BF16), col(b_in[dm:2 * dm]),
        w_in[:, o_g:o_u].T.astype(BF16), col(b_in[o_g:o_u]),
        w_in[:, o_u:o_ga].astype(BF16), row(b_in[o_u:o_ga]),
        p["hy_conv_w"][l], row(p["hy_conv_b"][l]))
    h_fwd, h_bwd = _mlstm(q, kt, v, g_col, g_row)
    y = _long_conv(zc, *filt)
    x = _merge(x, h_fwd, h_bwd, o, y, zc, x0, row(p["mlstm_norm"][l]), row(p["hy_skip"][l]),
               row(p["norm_mix"][l]),
               w_in[:, o_ga:o_ga + D].astype(BF16), row(b_in[o_ga:o_ga + D]),
               w_in[:, o_ga + D:].astype(BF16), row(b_in[o_ga + D:]),
               p["w_branch_a"][l].astype(BF16), p["w_branch_b"][l].astype(BF16),
               p["w_out"][l].astype(BF16))
    out = _mlp(x, row(p["norm_mlp"][l]), p["w_mlp1"][l].astype(BF16), p["w_mlp2"][l].astype(BF16),
               row(p["norm_final"]), l == depth - 1, host)
    return (out[0], out[1:]) if host is not None else (out, None)


def _filter_operands(p, l, L):
    n2 = 2 * L // DFT_N1
    m1f = jnp.asarray(_dft_tables(DFT_N1, n2)["m1f"]).astype(BF16)
    return _filter_args(m1f, p["hy_w1"][l], p["hy_b1"][l], p["hy_w2"][l], p["hy_b2"][l], p["hy_w3"][l],
                        p["hy_b3"][l], p["hy_freq"][l], p["hy_w_fo"][l], p["hy_decay"][l])


def _run(xs, p):
    depth = p["w_in"].shape[0]
    order = sorted(range(len(xs)), key=lambda gi: xs[gi].shape[1])
    blocks = [(gi, l) for l in range(depth) for gi in order]
    xs = list(xs)
    g0, l0 = blocks[0]
    filt = _filter_dft1(xs[g0].shape[1], _filter_operands(p, l0, xs[g0].shape[1]))
    for bi, (gi, l) in enumerate(blocks):
        host = None
        if bi + 1 < len(blocks):
            gn, ln = blocks[bi + 1]
            host = (xs[gn].shape[1], _filter_operands(p, ln, xs[gn].shape[1]))
        xs[gi], filt = _layer(xs[gi], p, l, filt, host)
    return tuple(xs)


def kernel(x_prompt, x_sample, norm_mix, w_in, b_in, mlstm_norm, hy_conv_w, hy_conv_b, hy_w1, hy_b1, hy_w2, hy_b2, hy_w3, hy_b3, hy_freq, hy_w_fo, hy_decay, hy_skip, w_branch_a, w_branch_b, w_out, norm_mlp, w_mlp1, w_mlp2, norm_final):
    p = dict(norm_mix=norm_mix, w_in=w_in, b_in=b_in, mlstm_norm=mlstm_norm, hy_conv_w=hy_conv_w,
             hy_conv_b=hy_conv_b, hy_w1=hy_w1, hy_b1=hy_b1, hy_w2=hy_w2, hy_b2=hy_b2, hy_w3=hy_w3,
             hy_b3=hy_b3, hy_freq=hy_freq, hy_w_fo=hy_w_fo, hy_decay=hy_decay, hy_skip=hy_skip,
             w_branch_a=w_branch_a, w_branch_b=w_branch_b, w_out=w_out, norm_mlp=norm_mlp,
             w_mlp1=w_mlp1, w_mlp2=w_mlp2, norm_final=norm_final)
    return _run((x_prompt, x_sample), p)
```

```python
import functools
import math

import numpy as np
import jax
import jax.numpy as jnp
from jax import lax
from jax.experimental import pallas as pl
from jax.experimental.pallas import tpu as pltpu

F32 = jnp.float32
BF16 = jnp.bfloat16

EPS = 1e-6
MLSTM_HEADS = 4
HEAD_DIM = 128
CHUNK = 256
HYENA_BANDS = 16
FILTER_HIDDEN = 64
DFT_N1 = 128
FILTER_ROWS = 512
SUBLANES = 8
TOKEN_BLOCK = 512
VMEM_LIMIT = 48 * 1024 * 1024
HIGHEST = lax.Precision.HIGHEST


def _params(*sem):
    return pltpu.CompilerParams(dimension_semantics=sem, vmem_limit_bytes=VMEM_LIMIT)


def _rms(x, g):
    return x * lax.rsqrt(jnp.mean(x * x, axis=-1, keepdims=True) + EPS) * g


def _dot(a, b):
    return jnp.dot(a, b, preferred_element_type=F32)


def _dot_nt(a, b):
    return lax.dot_general(a, b, (((1,), (1,)), ((), ())), preferred_element_type=F32)


def _split(a):
    hi = a.astype(BF16)
    return hi, (a - hi.astype(F32)).astype(BF16)


def _dot3(a, b_hi, b_lo):
    a_hi, a_lo = _split(a)
    return _dot(a_hi, b_hi) + (_dot(a_hi, b_lo) + _dot(a_lo, b_hi))


def _inproj_kernel(x_ref, xp_ref, xn_ref, g_ref, wm_ref, bm_ref, wkt_ref, bk_ref,
                   wgt_ref, bgt_ref, wu_ref, bu_ref, cw_ref, cb_ref,
                   q_ref, kt_ref, v_ref, o_ref, gc_ref, gr_ref, x0_ref, zc_ref):
    i = pl.program_id(1)
    last = pl.num_programs(1) - 1
    g = g_ref[...]
    n = _rms(x_ref[0], g).astype(BF16)
    dm = HEAD_DIM * MLSTM_HEADS
    pm = _dot(n, wm_ref[...]) + bm_ref[...]
    q_ref[0] = (pm[:, :dm] * (HEAD_DIM ** -0.5)).astype(BF16)
    v_ref[0] = pm[:, dm:2 * dm].astype(BF16)
    o_ref[0] = pm[:, 2 * dm:]
    kt_ref[0] = (_dot_nt(wkt_ref[...], n) + bk_ref[...]).astype(BF16)
    gates = _dot_nt(wgt_ref[...], n) + bgt_ref[...]
    gr_ref[0] = gates
    gc_ref[0] = gates.T

    tm = n.shape[0]
    halo = _rms(jnp.concatenate([xp_ref[0], xn_ref[0]], axis=0), g).astype(BF16)
    u_all = _dot(jnp.concatenate([n, halo], axis=0), wu_ref[...]) + bu_ref[...]
    u = u_all[:tm]
    u_prev = jnp.where(i > 0, u_all[tm + SUBLANES - 1:tm + SUBLANES], 0.0)
    u_next = jnp.where(i < last, u_all[tm + SUBLANES:tm + SUBLANES + 1], 0.0)
    row = lax.broadcasted_iota(jnp.int32, (tm, 1), 0)
    u_m1 = jnp.where(row == 0, u_prev, pltpu.roll(u, 1, 0))
    u_p1 = jnp.where(row == tm - 1, u_next, pltpu.roll(u, tm - 1, 0))
    cw = cw_ref[...]
    c = u_m1 * cw[0:1] + u * cw[1:2] + u_p1 * cw[2:3] + cb_ref[...]
    dh = c.shape[1] // 3
    x0_ref[0] = c[:, :dh]
    zc_ref[0] = c[:, dh:2 * dh] * c[:, 2 * dh:]


def _inproj(x, g, wm, bm, wkt, bk, wgt, bgt, wu, bu, cw, cb):
    B, L, D = x.shape
    tm = min(TOKEN_BLOCK, L)
    nb = L // tm
    r8 = tm // SUBLANES
    dm = wkt.shape[0]
    dh = wu.shape[1] // 3
    ng = wgt.shape[0]
    ws = (g, wm, bm, wkt, bk, wgt, bgt, wu, bu, cw, cb)
    full = lambda a: pl.BlockSpec(a.shape, lambda b, i: (0,) * a.ndim)
    tok = lambda w: pl.BlockSpec((1, tm, w), lambda b, i: (b, i, 0))
    tr = lambda w: pl.BlockSpec((1, w, tm), lambda b, i: (b, 0, i))
    in_specs = [
        tok(D),
        pl.BlockSpec((1, SUBLANES, D), lambda b, i: (b, jnp.maximum(i * r8 - 1, 0), 0)),
        pl.BlockSpec((1, SUBLANES, D), lambda b, i: (b, jnp.minimum((i + 1) * r8, L // SUBLANES - 1), 0)),
    ] + [full(w) for w in ws]
    sds = jax.ShapeDtypeStruct
    out_shape = (sds((B, L, dm), BF16), sds((B, dm, L), BF16), sds((B, L, dm), BF16), sds((B, L, dm), F32),
                 sds((B, L, ng), F32), sds((B, ng, L), F32), sds((B, L, dh), F32), sds((B, L, dh), F32))
    out_specs = (tok(dm), tr(dm), tok(dm), tok(dm), tok(ng), tr(ng), tok(dh), tok(dh))
    return pl.pallas_call(
        _inproj_kernel, grid=(B, nb), in_specs=in_specs, out_specs=out_specs,
        out_shape=out_shape, compiler_params=_params("parallel", "parallel"),
        name="inproj")(x, x, x, *ws)


def _log_sigmoid(x):
    return jnp.minimum(x, 0.0) - jnp.log1p(jnp.exp(-jnp.abs(x)))


def _mlstm_dir(reverse, q_ref, kt_ref, v_ref, gc_ref, gr_ref, out_ref, c_s, m_s, slot):
    nh = MLSTM_HEADS
    nb = q_ref.shape[0]
    nq = nb * nh
    ch = q_ref.shape[1]
    off = 2 * nh if reverse else 0
    ri = lax.broadcasted_iota(jnp.int32, (ch, ch), 0)
    ci = lax.broadcasted_iota(jnp.int32, (ch, ch), 1)
    mask = (ri <= ci) if reverse else (ci <= ri)
    tri_col = mask.astype(F32)
    tri_row = ((ci <= ri) if reverse else (ri <= ci)).astype(F32)

    li_col = jnp.concatenate([gc_ref[b][:, off:off + nh] for b in range(nb)], axis=1)
    lf_col = _log_sigmoid(jnp.concatenate([gc_ref[b][:, off + nh:off + 2 * nh] for b in range(nb)], axis=1))
    li_row = jnp.concatenate([gr_ref[b][off:off + nh, :] for b in range(nb)], axis=0)
    lf_row = _log_sigmoid(jnp.concatenate([gr_ref[b][off + nh:off + 2 * nh, :] for b in range(nb)], axis=0))
    b_col = jnp.dot(tri_col, lf_col, precision=HIGHEST, preferred_element_type=F32)
    b_row = jnp.dot(lf_row, tri_row, precision=HIGHEST, preferred_element_type=F32)
    beta_col = li_col - b_col
    beta_row = li_row - b_row
    cmx = beta_col
    rowi = lax.broadcasted_iota(jnp.int32, (ch, nq), 0)
    k = 1
    while k < ch:
        if reverse:
            cmx = jnp.maximum(cmx, jnp.where(rowi < ch - k, pltpu.roll(cmx, ch - k, 0), -jnp.inf))
        else:
            cmx = jnp.maximum(cmx, jnp.where(rowi >= k, pltpu.roll(cmx, k, 0), -jnp.inf))
        k *= 2
    last = 0 if reverse else ch - 1
    g_row = b_col[last:last + 1, :]
    m_loc = g_row + cmx[last:last + 1, :]
    m_prev = m_s[slot:slot + 1, 0:nq]
    mx = jnp.maximum(m_prev, cmx)
    s_inter = jnp.exp(m_prev - mx)
    inv_floor = jnp.exp(-mx - b_col)
    m_new = jnp.maximum(g_row + m_prev, m_loc)
    s_old = jnp.exp(g_row + m_prev - m_new)
    s_new = jnp.exp(m_loc - m_new)
    w_row = jnp.exp(beta_row - jnp.max(beta_row, axis=1, keepdims=True))
    m_s[slot:slot + 1, 0:nq] = m_new

    ones_col = (lax.broadcasted_iota(jnp.int32, (ch, HEAD_DIM), 1) == 0).astype(BF16)
    for b in range(nb):
        for h in range(nh):
            i = b * nh + h
            hs = slice(h * HEAD_DIM, (h + 1) * HEAD_DIM)
            q = q_ref[b, :, hs]
            kt = kt_ref[b, hs, :]
            v_aug = jnp.concatenate([v_ref[b, :, hs], ones_col], axis=1)
            d_mat = jnp.exp(jnp.where(mask, beta_row[i:i + 1, :] - mx[:, i:i + 1], -jnp.inf))
            scores = (_dot(q, kt) * d_mat).astype(BF16)
            c_prev = c_s[slot * nq + i]
            num = _dot(scores, v_aug) + s_inter[:, i:i + 1] * _dot(q, c_prev.astype(BF16))
            den = num[:, HEAD_DIM:HEAD_DIM + 1]
            out_ref[b, :, hs] = num[:, :HEAD_DIM] * (1.0 / jnp.maximum(jnp.abs(den), inv_floor[:, i:i + 1]))
            ktw = (kt.astype(F32) * w_row[i:i + 1, :]).astype(BF16)
            c_s[slot * nq + i] = s_old[:, i:i + 1] * c_prev + s_new[:, i:i + 1] * _dot(ktw, v_aug)


def _mlstm_kernel(qf, ktf, vf, gcf, grf, qb, ktb, vb, gcb, grb, of, ob, c_s, m_s):
    @pl.when(pl.program_id(0) == 0)
    def _():
        c_s[...] = jnp.zeros_like(c_s)
        m_s[...] = jnp.zeros_like(m_s)

    _mlstm_dir(False, qf, ktf, vf, gcf, grf, of, c_s, m_s, 0)
    _mlstm_dir(True, qb, ktb, vb, gcb, grb, ob, c_s, m_s, 1)


def _mlstm(q, kt, v, g_col, g_row):
    B, L, dm = q.shape
    ch = min(CHUNK, L)
    nc = L // ch
    specs = []
    for cidx in ((lambda c: c), (lambda c: nc - 1 - c)):
        tok = lambda w, cidx=cidx: pl.BlockSpec((B, ch, w), lambda c: (0, cidx(c), 0))
        tr = lambda w, cidx=cidx: pl.BlockSpec((B, w, ch), lambda c: (0, 0, cidx(c)))
        specs.append(([tok(dm), tr(dm), tok(dm), tok(g_col.shape[2]), tr(g_row.shape[1])], tok(dm)))
    args = [q, kt, v, g_col, g_row]
    out = jax.ShapeDtypeStruct((B, L, dm), F32)
    return pl.pallas_call(
        _mlstm_kernel, grid=(nc,), in_specs=specs[0][0] + specs[1][0],
        out_specs=(specs[0][1], specs[1][1]), out_shape=(out, out),
        scratch_shapes=[pltpu.VMEM((2 * B * MLSTM_HEADS, HEAD_DIM, 2 * HEAD_DIM), F32),
                        pltpu.VMEM((SUBLANES, 128), F32)],
        compiler_params=_params("arbitrary"), name="mlstm")(*args, *args)


N_FILTER_REFS = 16


def _filter_init(seq_len, refs, l1_ref, cb_s, sb_s):
    half = refs[0].shape[1] // 2
    hid = FILTER_HIDDEN
    grp = lax.broadcasted_iota(jnp.int32, (half, 2 * hid), 1) >= hid
    r2 = lax.broadcasted_iota(jnp.int32, (half, 2 * hid), 0)
    l1_ref[...] = jnp.zeros_like(l1_ref)
    lag0 = DFT_N1 * jnp.where(grp, half - r2, r2)
    ang = refs[1][...] * (lag0.astype(F32) * ((2.0 * math.pi) / seq_len))
    cb_s[...] = jnp.cos(ang)
    sb_s[...] = jnp.sin(ang)


def _filter_pass(seq_len, refs, n1_first, gs, store, l1_ref, cb_s, sb_s):
    (m_ref, fvec_ref, ph_ref, w1h_ref, w1l_ref, b1_ref, w2h_ref, w2l_ref, b2_ref, w3h_ref, w3l_ref,
     b3_ref, fr_ref, wfh_ref, wfl_ref, dec_ref) = refs
    half = m_ref.shape[1] // 2
    hid = FILTER_HIDDEN
    C = dec_ref.shape[1]
    L = seq_len
    lane = lax.broadcasted_iota(jnp.int32, (half, 2 * hid), 1)
    grp = lane >= hid
    feat = lane & (hid - 1)
    r2 = lax.broadcasted_iota(jnp.int32, (half, 2 * hid), 0)
    r_col = lax.broadcasted_iota(jnp.int32, (half, 1), 0)
    fvec = fvec_ref[...]
    fr = fr_ref[...]
    sign = jnp.where(grp[0:1], -1.0, 1.0)
    scale = (2.0 * math.pi) / L

    def lag(n):
        return jnp.where(n < L, n, jnp.where(n == L, 0, 2 * L - n)).astype(F32)

    grp_s = jnp.concatenate([grp] * gs, axis=0)
    zs = []
    for s in range(gs):
        n1 = n1_first + s
        n = n1 + DFT_N1 * (r2 + jnp.where(grp, half, 0))
        ang1 = sign * (fvec * (n1.astype(F32) * scale)) + ph_ref[...]
        feats = jnp.cos(ang1) * cb_s[...] - jnp.sin(ang1) * sb_s[...]
        feats = jnp.where(n == L, jnp.cos(ph_ref[...]), feats)
        zs.append(jnp.where(feat == 0, lag(n) / (L - 1),
                            jnp.where(feat <= 2 * HYENA_BANDS, feats, 0.0)))
    z = jnp.concatenate(zs, axis=0)
    h = jnp.sin(fr * (_dot3(z, w1h_ref[...], w1l_ref[...]) + b1_ref[...]))
    yield
    h = jnp.sin(fr * (_dot3(h, w2h_ref[...], w2l_ref[...]) + b2_ref[...]))
    yield
    h = jnp.sin(fr * (_dot3(h, w3h_ref[...], w3l_ref[...]) + b3_ref[...]))
    yield
    taps = []
    for gi in range(2):
        cs = slice(gi * C, (gi + 1) * C)
        hg = jnp.where(grp_s == (gi == 1), h, 0.0)
        tp = _dot3(hg, wfh_ref[:, cs], wfl_ref[:, cs])
        dec = jnp.abs(dec_ref[gi:gi + 1, :])
        parts = []
        for s in range(gs):
            n_col = n1_first + s + DFT_N1 * (r_col + gi * half)
            tps = tp[s * half:(s + 1) * half] * jnp.exp(-(lag(n_col) / (L - 1)) * dec)
            l1_ref[...] += jnp.sum(jnp.abs(tps), axis=0, keepdims=True)
            parts.append(jnp.where(n_col == L, 0.0, tps).astype(BF16))
        taps.append(parts)
    for s in range(gs):
        store(s, _pack(_dot(m_ref[...], jnp.concatenate([taps[0][s], taps[1][s]], axis=0))))


def _filter_dft1_kernel(seq_len, *refs):
    out_ref, l1_ref, cb_s, sb_s = refs[N_FILTER_REFS:]
    i = pl.program_id(0)
    nt = out_ref.shape[0]
    half = refs[0].shape[1] // 2
    gs = min(nt, max(1, FILTER_ROWS // half))

    @pl.when(i == 0)
    def _():
        _filter_init(seq_len, refs, l1_ref, cb_s, sb_s)

    def body(jg, carry):
        def store(s, val):
            out_ref[jg * gs + s] = val
        for _ in _filter_pass(seq_len, refs[:N_FILTER_REFS], i * nt + jg * gs, gs, store, l1_ref, cb_s, sb_s):
            pass
        return carry

    lax.fori_loop(0, nt // gs, body, 0)


def _blockdiag2(w):
    z = jnp.zeros_like(w)
    return jnp.concatenate([jnp.concatenate([w, z], axis=1), jnp.concatenate([z, w], axis=1)], axis=0)


def _filter_args(m1, w1, b1, w2, b2, w3, b3, fr, wfo, dec):
    hid = FILTER_HIDDEN
    fgrid = jnp.linspace(1e-4, HYENA_BANDS - 1, HYENA_BANDS, dtype=F32)
    fhalf = jnp.zeros((hid,), F32).at[1:1 + HYENA_BANDS].set(fgrid).at[1 + HYENA_BANDS:1 + 2 * HYENA_BANDS].set(fgrid)
    phalf = jnp.zeros((hid,), F32).at[1 + HYENA_BANDS:1 + 2 * HYENA_BANDS].set(0.5 * math.pi)
    two = lambda a: jnp.concatenate([a.reshape(1, -1), a.reshape(1, -1)], axis=1)
    w1p = jnp.zeros((hid, hid), F32).at[:w1.shape[0]].set(w1)
    mats = []
    for wmat in (_blockdiag2(w1p), _blockdiag2(w2), _blockdiag2(w3)):
        mats.append(_split(wmat))
    wst = jnp.concatenate([wfo, wfo], axis=0)
    wfh, wfl = _split(wst)
    args = (m1, two(fhalf), two(phalf), mats[0][0], mats[0][1], two(b1), mats[1][0], mats[1][1], two(b2),
            mats[2][0], mats[2][1], two(b3), two(fr), wfh, wfl, dec)
    assert len(args) == N_FILTER_REFS
    return args


def _filter_out(n2, C):
    shapes = (jax.ShapeDtypeStruct((DFT_N1, n2, C), jnp.int32), jax.ShapeDtypeStruct((1, C), F32))
    scratch = [pltpu.VMEM((n2 // 2, 2 * FILTER_HIDDEN), F32), pltpu.VMEM((n2 // 2, 2 * FILTER_HIDDEN), F32)]
    return shapes, scratch


def _filter_dft1(L, args):
    n2 = args[0].shape[1]
    C = args[-1].shape[1]
    full = lambda a: pl.BlockSpec(a.shape, lambda i: (0,) * a.ndim)
    nt = SUBLANES
    out_shape, scratch = _filter_out(n2, C)
    return pl.pallas_call(
        functools.partial(_filter_dft1_kernel, L),
        grid=(DFT_N1 // nt,), in_specs=[full(a) for a in args],
        out_specs=(pl.BlockSpec((nt, n2, C), lambda i: (i, 0, 0)), pl.BlockSpec((1, C), lambda i: (0, 0))),
        out_shape=out_shape, scratch_shapes=scratch,
        compiler_params=_params("arbitrary"),
        name="filter_dft1")(*args)


@functools.lru_cache(maxsize=None)
def _dft_tables(n1, n2):
    n = n1 * n2
    h = n2 // 2
    a2 = 2.0 * np.pi * np.outer(np.arange(n2), np.arange(n2)) / n2
    c2, s2 = np.cos(a2), np.sin(a2)
    m1d = np.empty((n2, 2, n2))
    m1d[:, 0, :h], m1d[:, 0, h:] = c2[:, :h], s2[:, :h]
    m1d[:, 1, :h], m1d[:, 1, h:] = -s2[:, :h], c2[:, :h]
    m1f = np.stack([c2, -s2], axis=1)
    a1 = 2.0 * np.pi * np.outer(np.arange(n1), np.arange(n1)) / n1
    f1r, f1i = np.cos(a1), -np.sin(a1)
    il = lambda m: m.reshape(m.shape[0], 2, -1).swapaxes(1, 2).reshape(m.shape[0], -1)
    pg = il(np.block([[f1r, -f1i], [f1i, f1r]]))
    pgs = il(np.block([[-f1i, -f1r], [f1r, -f1i]]))
    qd = il(np.block([[f1r, f1i], [-f1i, f1r]]).T).T
    at = 2.0 * np.pi * np.outer(np.arange(n2), np.arange(n1)) / n
    tr, ti = np.cos(at), -np.sin(at)
    txr = np.repeat(tr, 2, axis=1)
    txi = np.repeat(ti, 2, axis=1)
    er, ei = c2[:h] / n, s2[:h] / n
    pp = il(np.block([[er, -ei], [ei, er]]))
    pps = il(np.block([[-ei, -er], [er, -ei]]))
    ur, ui = np.cos(at).T, np.sin(at).T
    uxr = np.repeat(ur, 2, axis=1)
    uxi = np.repeat(ui, 2, axis=1)
    f32 = lambda a: np.asarray(a, np.float32)
    return dict(m1d=f32(m1d.reshape(2 * n2, n2)), m1f=f32(m1f.reshape(2 * n2, n2)),
                pg=f32(pg), pgs=f32(pgs), qd=f32(qd), txr=f32(txr), txi=f32(txi),
                pp=f32(pp), pps=f32(pps), uxr=f32(uxr), uxi=f32(uxi))


def _pack(x):
    return pltpu.bitcast(x.astype(BF16), jnp.int32)


def _unpack(w):
    return pltpu.bitcast(w, BF16)


def _dft1_kernel(m_ref, x_ref, out_ref):
    nt = out_ref.shape[0]
    m = m_ref[...]
    for j in range(nt):
        parts = [x_ref[b, :, j, :] for b in range(x_ref.shape[0])]
        x = parts[0] if len(parts) == 1 else jnp.concatenate(parts, axis=0)
        out_ref[j] = _pack(_dot(m, x.astype(BF16)))


def _dft1(m1, x):
    parts, rows, n1, C = x.shape
    nt = SUBLANES
    return pl.pallas_call(
        _dft1_kernel, grid=(n1 // nt,),
        in_specs=[pl.BlockSpec(m1.shape, lambda i: (0, 0)),
                  pl.BlockSpec((parts, rows, nt, C), lambda i: (0, 0, i, 0))],
        out_specs=pl.BlockSpec((nt, m1.shape[0] // 2, C), lambda i: (i, 0, 0)),
        out_shape=jax.ShapeDtypeStruct((n1, m1.shape[0] // 2, C), jnp.int32),
        compiler_params=_params("parallel"), name="dft_stage1")(m1, x)


def _conv_mid_kernel(a_ref, af_ref, l1_ref, pg_ref, pgs_ref, txr_ref, txi_ref, qd_ref, out_ref):
    n1 = a_ref.shape[0]
    qd = qd_ref[...]
    inv = 1.0 / l1_ref[...]
    for j in range(out_ref.shape[0]):
        gd = (pg_ref[...] * txr_ref[j:j + 1, :] + pgs_ref[...] * txi_ref[j:j + 1, :]).astype(BF16)
        x = _dot(gd, _unpack(a_ref[:, j, :]))
        kf = _dot(gd, _unpack(af_ref[:, j, :])) * inv
        xr, xi = x[:n1], x[n1:]
        kr, ki = kf[:n1], kf[n1:]
        y = jnp.concatenate([xr * kr - xi * ki, xr * ki + xi * kr], axis=0).astype(BF16)
        out_ref[j] = _pack(_dot(qd, y))


def _conv_mid(a, af, l1, tabs):
    n1, n2, C = a.shape
    kt = SUBLANES
    ablk = pl.BlockSpec((n1, kt, C), lambda i: (0, i, 0))
    mat = pl.BlockSpec((2 * n1, 2 * n1), lambda i: (0, 0))
    tw = pl.BlockSpec((kt, 2 * n1), lambda i: (i, 0))
    return pl.pallas_call(
        _conv_mid_kernel, grid=(n2 // kt,),
        in_specs=[ablk, ablk, pl.BlockSpec((1, C), lambda i: (0, 0)), mat, mat, tw, tw, mat],
        out_specs=pl.BlockSpec((kt, n1, C), lambda i: (i, 0, 0)),
        out_shape=jax.ShapeDtypeStruct((n2, n1, C), jnp.int32),
        compiler_params=_params("parallel"), name="conv_mid")(
            a, af, l1, tabs["pg"], tabs["pgs"], tabs["txr"], tabs["txi"], tabs["qd"])


def _conv_out_kernel(b_ref, pp_ref, pps_ref, uxr_ref, uxi_ref, out_ref):
    h = out_ref.shape[1]
    for j in range(out_ref.shape[2]):
        hm = (pp_ref[...] * uxr_ref[j:j + 1, :] + pps_ref[...] * uxi_ref[j:j + 1, :]).astype(BF16)
        y = _dot(hm, _unpack(b_ref[:, j, :]))
        out_ref[0, :, j, :] = y[:h]
        out_ref[1, :, j, :] = y[h:]


def _conv_out(bm, tabs):
    n2, n1, C = bm.shape
    nt = SUBLANES
    mat = pl.BlockSpec((n2, 2 * n2), lambda i: (0, 0))
    tw = pl.BlockSpec((nt, 2 * n2), lambda i: (i, 0))
    return pl.pallas_call(
        _conv_out_kernel, grid=(n1 // nt,),
        in_specs=[pl.BlockSpec((n2, nt, C), lambda i: (0, i, 0)), mat, mat, tw, tw],
        out_specs=pl.BlockSpec((2, n2 // 2, nt, C), lambda i: (0, 0, i, 0)),
        out_shape=jax.ShapeDtypeStruct((2, n2 // 2, n1, C), F32),
        compiler_params=_params("parallel"), name="conv_out")(
            bm, tabs["pp"], tabs["pps"], tabs["uxr"], tabs["uxi"])


def _long_conv(zc, af, l1):
    B, L, C = zc.shape
    assert B == 2, "the two batch rows are packed as one complex sequence"
    n1 = DFT_N1
    n2 = 2 * L // n1
    tabs = {k: jnp.asarray(v) for k, v in _dft_tables(n1, n2).items()}
    a = _dft1(tabs["m1d"].astype(BF16), zc.reshape(B, n2 // 2, n1, C))
    bm = _conv_mid(a, af, l1, tabs)
    return _conv_out(bm, tabs).reshape(B, L, C)


def _merge_kernel(x_ref, hf_ref, hb_ref, o_ref, y_ref, zc_ref, x0_ref, hn_ref, skip_ref, g_ref,
                  wga_ref, bga_ref, wgb_ref, bgb_ref, wa_ref, wb_ref, wo_ref, out_ref):
    x = x_ref[0]
    n = _rms(x, g_ref[...]).astype(BF16)
    ga = jax.nn.sigmoid(_dot(n, wga_ref[...]) + bga_ref[...])
    gb = jax.nn.sigmoid(_dot(n, wgb_ref[...]) + bgb_ref[...])
    heads = []
    for h in range(MLSTM_HEADS):
        hs = slice(h * HEAD_DIM, (h + 1) * HEAD_DIM)
        s = hf_ref[0, :, hs] + hb_ref[0, :, hs]
        dv = s - jnp.mean(s, axis=1, keepdims=True)
        var = jnp.mean(dv * dv, axis=1, keepdims=True)
        heads.append((dv * lax.rsqrt(var + EPS) * hn_ref[:, hs]
                      * jax.nn.sigmoid(o_ref[0, :, hs])).astype(BF16))
    ml = jnp.concatenate(heads, axis=1)
    hy = (x0_ref[0] * (y_ref[0] + skip_ref[...] * zc_ref[0])).astype(BF16)
    merged = ga * _dot(ml, wa_ref[...]) + gb * _dot(hy, wb_ref[...])
    out_ref[0] = x + _dot(merged.astype(BF16), wo_ref[...])


def _merge(x, hf, hb, o, y, zc, x0, hn, skip, g, wga, bga, wgb, bgb, wa, wb, wo):
    B, L, D = x.shape
    tm = min(TOKEN_BLOCK, L)
    full = lambda w: pl.BlockSpec(w.shape, lambda bb, i: (0,) * w.ndim)
    tok = lambda w: pl.BlockSpec((1, tm, w), lambda bb, i: (bb, i, 0))
    ws = (hn, skip, g, wga, bga, wgb, bgb, wa, wb, wo)
    acts = (hf, hb, o, y, zc, x0)
    return pl.pallas_call(
        _merge_kernel, grid=(B, L // tm),
        in_specs=[tok(D)] + [tok(a.shape[2]) for a in acts] + [full(w) for w in ws],
        out_specs=tok(D), out_shape=jax.ShapeDtypeStruct(x.shape, F32),
        compiler_params=_params("parallel", "parallel"), name="merge")(x, *acts, *ws)


def _mlp_kernel(final, nchunk, host_len, x_ref, g_ref, w1_ref, w2_ref, gf_ref, *rest):
    out_ref = rest[N_FILTER_REFS] if host_len else rest[0]
    if host_len:
        af_ref, l1_ref, cb_s, sb_s = rest[N_FILTER_REFS + 1:]
        step = pl.program_id(0) * pl.num_programs(1) + pl.program_id(1)

        @pl.when(step == 0)
        def _():
            _filter_init(host_len, rest, l1_ref, cb_s, sb_s)

        gs = af_ref.shape[0]

        def store(s, val):
            af_ref[s] = val
        hosted = _filter_pass(host_len, rest[:N_FILTER_REFS], step * gs, gs, store, l1_ref, cb_s, sb_s)
    else:
        hosted = iter(())

    x = x_ref[0]
    n = _rms(x, g_ref[...]).astype(BF16)
    fc = w1_ref.shape[1] // nchunk
    acc = x
    for c in range(nchunk):
        next(hosted, None)
        hcl = jnp.maximum(_dot(n, w1_ref[:, c * fc:(c + 1) * fc]), 0.0)
        acc = acc + _dot((hcl * hcl).astype(BF16), w2_ref[c * fc:(c + 1) * fc, :])
    for _ in hosted:
        pass
    out_ref[0] = _rms(acc, gf_ref[...]) if final else acc


def _mlp(x, g, w1, w2, gf, final, host=None):
    B, L, D = x.shape
    tm = min(TOKEN_BLOCK, L)
    steps = B * (L // tm)
    full = lambda w: pl.BlockSpec(w.shape, lambda bb, i: (0,) * w.ndim)
    once = lambda w: pl.BlockSpec(w.shape, lambda bb, i: (0,) * w.ndim, pipeline_mode=pl.Buffered(1))
    tok = pl.BlockSpec((1, tm, D), lambda bb, i: (bb, i, 0))
    args = [x, g, w1, w2, gf]
    in_specs = [tok, full(g), once(w1), once(w2), full(gf)]
    out_specs, out_shape, scratch = tok, jax.ShapeDtypeStruct(x.shape, F32), []
    sem = ("parallel", "parallel")
    host_len = 0
    if host is not None:
        host_len, fargs = host
        n2 = fargs[0].shape[1]
        C = fargs[-1].shape[1]
        gs = DFT_N1 // steps
        assert gs * steps == DFT_N1
        args += list(fargs)
        in_specs += [full(a) for a in fargs]
        per_l = L // tm
        fshape, scratch = _filter_out(n2, C)
        out_specs = (tok, pl.BlockSpec((gs, n2, C), lambda bb, i: (bb * per_l + i, 0, 0)),
                     pl.BlockSpec((1, C), lambda bb, i: (0, 0)))
        out_shape = (out_shape,) + fshape
        sem = ("arbitrary", "arbitrary")
    return pl.pallas_call(
        functools.partial(_mlp_kernel, final, 4, host_len), grid=(B, L // tm),
        in_specs=in_specs, out_specs=out_specs, out_shape=out_shape, scratch_shapes=scratch,
        compiler_params=_params(*sem), name="mlp")(*args)


def _layer(x, p, l, filt, host):
    depth = p["w_in"].shape[0]
    D = x.shape[2]
    dm = HEAD_DIM * MLSTM_HEADS
    ng = 4 * MLSTM_HEADS
    dh = p["hy_skip"].shape[1]
    o_g, o_u, o_ga = 4 * dm, 4 * dm + ng, 4 * dm + ng + 3 * dh
    row = lambda a: a.reshape(1, -1)
    col = lambda a: a.reshape(-1, 1)
    w_in = p["w_in"][l]
    b_in = p["b_in"][l]
    w_qvo = jnp.concatenate([w_in[:, :dm], w_in[:, 2 * dm:4 * dm]], axis=1)
    b_qvo = jnp.concatenate([b_in[:dm], b_in[2 * dm:4 * dm]])
    q, kt, v, o, g_col, g_row, x0, zc = _inproj(
        x, row(p["norm_mix"][l]), w_qvo.astype(BF16), row(b_qvo),
        w_in[:, dm:2 * dm].T.astype(BF16), col(b_in[dm:2 * dm]),
        w_in[:, o_g:o_u].T.astype(BF16), col(b_in[o_g:o_u]),
        w_in[:, o_u:o_ga].astype(BF16), row(b_in[o_u:o_ga]),
        p["hy_conv_w"][l], row(p["hy_conv_b"][l]))
    h_fwd, h_bwd = _mlstm(q, kt, v, g_col, g_row)
    y = _long_conv(zc, *filt)
    x = _merge(x, h_fwd, h_bwd, o, y, zc, x0, row(p["mlstm_norm"][l]), row(p["hy_skip"][l]),
               row(p["norm_mix"][l]),
               w_in[:, o_ga:o_ga + D].astype(BF16), row(b_in[o_ga:o_ga + D]),
               w_in[:, o_ga + D:].astype(BF16), row(b_in[o_ga + D:]),
               p["w_branch_a"][l].astype(BF16), p["w_branch_b"][l].astype(BF16),
               p["w_out"][l].astype(BF16))
    out = _mlp(x, row(p["norm_mlp"][l]), p["w_mlp1"][l].astype(BF16), p["w_mlp2"][l].astype(BF16),
               row(p["norm_final"]), l == depth - 1, host)
    return (out[0], out[1:]) if host is not None else (out, None)


def _filter_operands(p, l, L):
    n2 = 2 * L // DFT_N1
    m1f = jnp.asarray(_dft_tables(DFT_N1, n2)["m1f"]).astype(BF16)
    return _filter_args(m1f, p["hy_w1"][l], p["hy_b1"][l], p["hy_w2"][l], p["hy_b2"][l], p["hy_w3"][l],
                        p["hy_b3"][l], p["hy_freq"][l], p["hy_w_fo"][l], p["hy_decay"][l])


def _run(xs, p):
    depth = p["w_in"].shape[0]
    order = sorted(range(len(xs)), key=lambda gi: xs[gi].shape[1])
    blocks = [(gi, l) for l in range(depth) for gi in order]
    xs = list(xs)
    g0, l0 = blocks[0]
    filt = _filter_dft1(xs[g0].shape[1], _filter_operands(p, l0, xs[g0].shape[1]))
    for bi, (gi, l) in enumerate(blocks):
        host = None
        if bi + 1 < len(blocks):
            gn, ln = blocks[bi + 1]
            host = (xs[gn].shape[1], _filter_operands(p, ln, xs[gn].shape[1]))
        xs[gi], filt = _layer(xs[gi], p, l, filt, host)
    return tuple(xs)


def kernel(x_prompt, x_sample, norm_mix, w_in, b_in, mlstm_norm, hy_conv_w, hy_conv_b, hy_w1, hy_b1, hy_w2, hy_b2, hy_w3, hy_b3, hy_freq, hy_w_fo, hy_decay, hy_skip, w_branch_a, w_branch_b, w_out, norm_mlp, w_mlp1, w_mlp2, norm_final):
    p = dict(norm_mix=norm_mix, w_in=w_in, b_in=b_in, mlstm_norm=mlstm_norm, hy_conv_w=hy_conv_w,
             hy_conv_b=hy_conv_b, hy_w1=hy_w1, hy_b1=hy_b1, hy_w2=hy_w2, hy_b2=hy_b2, hy_w3=hy_w3,
             hy_b3=hy_b3, hy_freq=hy_freq, hy_w_fo=hy_w_fo, hy_decay=hy_decay, hy_skip=hy_skip,
             w_branch_a=w_branch_a, w_branch_b=w_branch_b, w_out=w_out, norm_mlp=norm_mlp,
             w_mlp1=w_mlp1, w_mlp2=w_mlp2, norm_final=norm_final)
    return _run((x_prompt, x_sample), p)
```

```python
import functools
import math

import numpy as np
import jax
import jax.numpy as jnp
from jax import lax
from jax.experimental import pallas as pl
from jax.experimental.pallas import tpu as pltpu

F32 = jnp.float32
BF16 = jnp.bfloat16

EPS = 1e-6
MLSTM_HEADS = 4
HEAD_DIM = 128
CHUNK = 256
HYENA_BANDS = 16
FILTER_HIDDEN = 64
DFT_N1 = 128
FILTER_ROWS = 512
SUBLANES = 8
TOKEN_BLOCK = 512
VMEM_LIMIT = 48 * 1024 * 1024
HIGHEST = lax.Precision.HIGHEST


def _params(*sem):
    return pltpu.CompilerParams(dimension_semantics=sem, vmem_limit_bytes=VMEM_LIMIT)


def _rms(x, g):
    return x * lax.rsqrt(jnp.mean(x * x, axis=-1, keepdims=True) + EPS) * g


def _dot(a, b):
    return jnp.dot(a, b, preferred_element_type=F32)


def _dot_nt(a, b):
    return lax.dot_general(a, b, (((1,), (1,)), ((), ())), preferred_element_type=F32)


def _split(a):
    hi = a.astype(BF16)
    return hi, (a - hi.astype(F32)).astype(BF16)


def _dot3(a, b_hi, b_lo):
    a_hi, a_lo = _split(a)
    return _dot(a_hi, b_hi) + (_dot(a_hi, b_lo) + _dot(a_lo, b_hi))


def _inproj_kernel(x_ref, xp_ref, xn_ref, g_ref, wm_ref, bm_ref, wkt_ref, bk_ref,
                   wgt_ref, bgt_ref, wu_ref, bu_ref, cw_ref, cb_ref,
                   q_ref, kt_ref, v_ref, o_ref, gc_ref, gr_ref, x0_ref, zc_ref):
    i = pl.program_id(1)
    last = pl.num_programs(1) - 1
    g = g_ref[...]
    n = _rms(x_ref[0], g).astype(BF16)
    dm = HEAD_DIM * MLSTM_HEADS
    pm = _dot(n, wm_ref[...]) + bm_ref[...]
    q_ref[0] = (pm[:, :dm] * (HEAD_DIM ** -0.5)).astype(BF16)
    v_ref[0] = pm[:, dm:2 * dm].astype(BF16)
    o_ref[0] = pm[:, 2 * dm:]
    kt_ref[0] = (_dot_nt(wkt_ref[...], n) + bk_ref[...]).astype(BF16)
    gates = _dot_nt(wgt_ref[...], n) + bgt_ref[...]
    gr_ref[0] = gates
    gc_ref[0] = gates.T

    tm = n.shape[0]
    halo = _rms(jnp.concatenate([xp_ref[0], xn_ref[0]], axis=0), g).astype(BF16)
    u_all = _dot(jnp.concatenate([n, halo], axis=0), wu_ref[...]) + bu_ref[...]
    u = u_all[:tm]
    u_prev = jnp.where(i > 0, u_all[tm + SUBLANES - 1:tm + SUBLANES], 0.0)
    u_next = jnp.where(i < last, u_all[tm + SUBLANES:tm + SUBLANES + 1], 0.0)
    row = lax.broadcasted_iota(jnp.int32, (tm, 1), 0)
    u_m1 = jnp.where(row == 0, u_prev, pltpu.roll(u, 1, 0))
    u_p1 = jnp.where(row == tm - 1, u_next, pltpu.roll(u, tm - 1, 0))
    cw = cw_ref[...]
    c = u_m1 * cw[0:1] + u * cw[1:2] + u_p1 * cw[2:3] + cb_ref[...]
    dh = c.shape[1] // 3
    x0_ref[0] = c[:, :dh]
    zc_ref[0] = c[:, dh:2 * dh] * c[:, 2 * dh:]


def _inproj(x, g, wm, bm, wkt, bk, wgt, bgt, wu, bu, cw, cb):
    B, L, D = x.shape
    tm = min(TOKEN_BLOCK, L)
    nb = L // tm
    r8 = tm // SUBLANES
    dm = wkt.shape[0]
    dh = wu.shape[1] // 3
    ng = wgt.shape[0]
    ws = (g, wm, bm, wkt, bk, wgt, bgt, wu, bu, cw, cb)
    full = lambda a: pl.BlockSpec(a.shape, lambda b, i: (0,) * a.ndim)
    tok = lambda w: pl.BlockSpec((1, tm, w), lambda b, i: (b, i, 0))
    tr = lambda w: pl.BlockSpec((1, w, tm), lambda b, i: (b, 0, i))
    in_specs = [
        tok(D),
        pl.BlockSpec((1, SUBLANES, D), lambda b, i: (b, jnp.maximum(i * r8 - 1, 0), 0)),
        pl.BlockSpec((1, SUBLANES, D), lambda b, i: (b, jnp.minimum((i + 1) * r8, L // SUBLANES - 1), 0)),
    ] + [full(w) for w in ws]
    sds = jax.ShapeDtypeStruct
    out_shape = (sds((B, L, dm), BF16), sds((B, dm, L), BF16), sds((B, L, dm), BF16), sds((B, L, dm), F32),
                 sds((B, L, ng), F32), sds((B, ng, L), F32), sds((B, L, dh), F32), sds((B, L, dh), F32))
    out_specs = (tok(dm), tr(dm), tok(dm), tok(dm), tok(ng), tr(ng), tok(dh), tok(dh))
    return pl.pallas_call(
        _inproj_kernel, grid=(B, nb), in_specs=in_specs, out_specs=out_specs,
        out_shape=out_shape, compiler_params=_params("parallel", "parallel"),
        name="inproj")(x, x, x, *ws)


def _log_sigmoid(x):
    return jnp.minimum(x, 0.0) - jnp.log1p(jnp.exp(-jnp.abs(x)))


def _mlstm_dir(reverse, q_ref, kt_ref, v_ref, gc_ref, gr_ref, out_ref, c_s, m_s, slot):
    nh = MLSTM_HEADS
    nb = q_ref.shape[0]
    nq = nb * nh
    ch = q_ref.shape[1]
    off = 2 * nh if reverse else 0
    ri = lax.broadcasted_iota(jnp.int32, (ch, ch), 0)
    ci = lax.broadcasted_iota(jnp.int32, (ch, ch), 1)
    mask = (ri <= ci) if reverse else (ci <= ri)
    tri_row = ((ci <= ri) if reverse else (ri <= ci)).astype(F32)

    li_col = jnp.concatenate([gc_ref[b][:, off:off + nh] for b in range(nb)], axis=1)
    li_row = jnp.concatenate([gr_ref[b][off:off + nh, :] for b in range(nb)], axis=0)
    lf_row = _log_sigmoid(jnp.concatenate([gr_ref[b][off + nh:off + 2 * nh, :] for b in range(nb)], axis=0))
    b_row = jnp.dot(lf_row, tri_row, precision=HIGHEST, preferred_element_type=F32)
    b_col = b_row.T
    beta_col = li_col - b_col
    beta_row = li_row - b_row
    cmx = beta_col
    rowi = lax.broadcasted_iota(jnp.int32, (ch, nq), 0)
    k = 1
    while k < ch:
        if reverse:
            cmx = jnp.maximum(cmx, jnp.where(rowi < ch - k, pltpu.roll(cmx, ch - k, 0), -jnp.inf))
        else:
            cmx = jnp.maximum(cmx, jnp.where(rowi >= k, pltpu.roll(cmx, k, 0), -jnp.inf))
        k *= 2
    last = 0 if reverse else ch - 1
    g_row = b_col[last:last + 1, :]
    m_loc = g_row + cmx[last:last + 1, :]
    m_prev = m_s[slot:slot + 1, 0:nq]
    mx = jnp.maximum(m_prev, cmx)
    s_inter = jnp.exp(m_prev - mx)
    inv_floor = jnp.exp(-mx - b_col)
    m_new = jnp.maximum(g_row + m_prev, m_loc)
    s_old = jnp.exp(g_row + m_prev - m_new)
    s_new = jnp.exp(m_loc - m_new)
    w_row = jnp.exp(beta_row - jnp.max(beta_row, axis=1, keepdims=True))
    m_s[slot:slot + 1, 0:nq] = m_new

    ones_col = (lax.broadcasted_iota(jnp.int32, (ch, HEAD_DIM), 1) == 0).astype(BF16)
    for b in range(nb):
        for h in range(nh):
            i = b * nh + h
            hs = slice(h * HEAD_DIM, (h + 1) * HEAD_DIM)
            q = q_ref[b, :, hs]
            kt = kt_ref[b, hs, :]
            v_aug = jnp.concatenate([v_ref[b, :, hs], ones_col], axis=1)
            d_mat = jnp.exp(jnp.where(mask, beta_row[i:i + 1, :] - mx[:, i:i + 1], -jnp.inf))
            scores = (_dot(q, kt) * d_mat).astype(BF16)
            c_prev = c_s[slot * nq + i]
            num = _dot(scores, v_aug) + s_inter[:, i:i + 1] * _dot(q, c_prev.astype(BF16))
            den = num[:, HEAD_DIM:HEAD_DIM + 1]
            out_ref[b, :, hs] = num[:, :HEAD_DIM] * (1.0 / jnp.maximum(jnp.abs(den), inv_floor[:, i:i + 1]))
            ktw = (kt.astype(F32) * w_row[i:i + 1, :]).astype(BF16)
            c_s[slot * nq + i] = s_old[:, i:i + 1] * c_prev + s_new[:, i:i + 1] * _dot(ktw, v_aug)


def _mlstm_kernel(qf, ktf, vf, gcf, grf, qb, ktb, vb, gcb, grb, of, ob, c_s, m_s):
    @pl.when(pl.program_id(0) == 0)
    def _():
        c_s[...] = jnp.zeros_like(c_s)
        m_s[...] = jnp.zeros_like(m_s)

    _mlstm_dir(False, qf, ktf, vf, gcf, grf, of, c_s, m_s, 0)
    _mlstm_dir(True, qb, ktb, vb, gcb, grb, ob, c_s, m_s, 1)


def _mlstm(q, kt, v, g_col, g_row):
    B, L, dm = q.shape
    ch = min(CHUNK, L)
    nc = L // ch
    specs = []
    for cidx in ((lambda c: c), (lambda c: nc - 1 - c)):
        tok = lambda w, cidx=cidx: pl.BlockSpec((B, ch, w), lambda c: (0, cidx(c), 0))
        tr = lambda w, cidx=cidx: pl.BlockSpec((B, w, ch), lambda c: (0, 0, cidx(c)))
        specs.append(([tok(dm), tr(dm), tok(dm), tok(g_col.shape[2]), tr(g_row.shape[1])], tok(dm)))
    args = [q, kt, v, g_col, g_row]
    out = jax.ShapeDtypeStruct((B, L, dm), F32)
    return pl.pallas_call(
        _mlstm_kernel, grid=(nc,), in_specs=specs[0][0] + specs[1][0],
        out_specs=(specs[0][1], specs[1][1]), out_shape=(out, out),
        scratch_shapes=[pltpu.VMEM((2 * B * MLSTM_HEADS, HEAD_DIM, 2 * HEAD_DIM), F32),
                        pltpu.VMEM((SUBLANES, 128), F32)],
        compiler_params=_params("arbitrary"), name="mlstm")(*args, *args)


N_FILTER_REFS = 16


def _filter_init(seq_len, refs, l1_ref, cb_s, sb_s):
    half = refs[0].shape[1] // 2
    hid = FILTER_HIDDEN
    grp = lax.broadcasted_iota(jnp.int32, (half, 2 * hid), 1) >= hid
    r2 = lax.broadcasted_iota(jnp.int32, (half, 2 * hid), 0)
    l1_ref[...] = jnp.zeros_like(l1_ref)
    lag0 = DFT_N1 * jnp.where(grp, half - r2, r2)
    ang = refs[1][...] * (lag0.astype(F32) * ((2.0 * math.pi) / seq_len))
    cb_s[...] = jnp.cos(ang)
    sb_s[...] = jnp.sin(ang)


def _filter_pass(seq_len, refs, n1_first, gs, store, l1_ref, cb_s, sb_s):
    (m_ref, fvec_ref, ph_ref, w1h_ref, w1l_ref, b1_ref, w2h_ref, w2l_ref, b2_ref, w3h_ref, w3l_ref,
     b3_ref, fr_ref, wfh_ref, wfl_ref, dec_ref) = refs
    half = m_ref.shape[1] // 2
    hid = FILTER_HIDDEN
    C = dec_ref.shape[1]
    L = seq_len
    lane = lax.broadcasted_iota(jnp.int32, (half, 2 * hid), 1)
    grp = lane >= hid
    feat = lane & (hid - 1)
    r2 = lax.broadcasted_iota(jnp.int32, (half, 2 * hid), 0)
    r_col = lax.broadcasted_iota(jnp.int32, (half, 1), 0)
    fvec = fvec_ref[...]
    fr = fr_ref[...]
    sign = jnp.where(grp[0:1], -1.0, 1.0)
    scale = (2.0 * math.pi) / L

    def lag(n):
        return jnp.where(n < L, n, jnp.where(n == L, 0, 2 * L - n)).astype(F32)

    grp_s = jnp.concatenate([grp] * gs, axis=0)
    zs = []
    for s in range(gs):
        n1 = n1_first + s
        n = n1 + DFT_N1 * (r2 + jnp.where(grp, half, 0))
        ang1 = sign * (fvec * (n1.astype(F32) * scale)) + ph_ref[...]
        feats = jnp.cos(ang1) * cb_s[...] - jnp.sin(ang1) * sb_s[...]
        feats = jnp.where(n == L, jnp.cos(ph_ref[...]), feats)
        zs.append(jnp.where(feat == 0, lag(n) / (L - 1),
                            jnp.where(feat <= 2 * HYENA_BANDS, feats, 0.0)))
    z = jnp.concatenate(zs, axis=0)
    h = jnp.sin(fr * (_dot3(z, w1h_ref[...], w1l_ref[...]) + b1_ref[...]))
    yield
    h = jnp.sin(fr * (_dot3(h, w2h_ref[...], w2l_ref[...]) + b2_ref[...]))
    yield
    h = jnp.sin(fr * (_dot3(h, w3h_ref[...], w3l_ref[...]) + b3_ref[...]))
    yield
    taps = []
    for gi in range(2):
        cs = slice(gi * C, (gi + 1) * C)
        hg = jnp.where(grp_s == (gi == 1), h, 0.0)
        tp = _dot3(hg, wfh_ref[:, cs], wfl_ref[:, cs])
        dec = jnp.abs(dec_ref[gi:gi + 1, :])
        parts = []
        for s in range(gs):
            n_col = n1_first + s + DFT_N1 * (r_col + gi * half)
            tps = tp[s * half:(s + 1) * half] * jnp.exp(-(lag(n_col) / (L - 1)) * dec)
            l1_ref[...] += jnp.sum(jnp.abs(tps), axis=0, keepdims=True)
            parts.append(jnp.where(n_col == L, 0.0, tps).astype(BF16))
        taps.append(parts)
    for s in range(gs):
        store(s, _pack(_dot(m_ref[...], jnp.concatenate([taps[0][s], taps[1][s]], axis=0))))


def _filter_dft1_kernel(seq_len, *refs):
    out_ref, l1_ref, cb_s, sb_s = refs[N_FILTER_REFS:]
    i = pl.program_id(0)
    nt = out_ref.shape[0]
    half = refs[0].shape[1] // 2
    gs = min(nt, max(1, FILTER_ROWS // half))

    @pl.when(i == 0)
    def _():
        _filter_init(seq_len, refs, l1_ref, cb_s, sb_s)

    def body(jg, carry):
        def store(s, val):
            out_ref[jg * gs + s] = val
        for _ in _filter_pass(seq_len, refs[:N_FILTER_REFS], i * nt + jg * gs, gs, store, l1_ref, cb_s, sb_s):
            pass
        return carry

    lax.fori_loop(0, nt // gs, body, 0)


def _blockdiag2(w):
    z = jnp.zeros_like(w)
    return jnp.concatenate([jnp.concatenate([w, z], axis=1), jnp.concatenate([z, w], axis=1)], axis=0)


def _filter_args(m1, w1, b1, w2, b2, w3, b3, fr, wfo, dec):
    hid = FILTER_HIDDEN
    fgrid = jnp.linspace(1e-4, HYENA_BANDS - 1, HYENA_BANDS, dtype=F32)
    fhalf = jnp.zeros((hid,), F32).at[1:1 + HYENA_BANDS].set(fgrid).at[1 + HYENA_BANDS:1 + 2 * HYENA_BANDS].set(fgrid)
    phalf = jnp.zeros((hid,), F32).at[1 + HYENA_BANDS:1 + 2 * HYENA_BANDS].set(0.5 * math.pi)
    two = lambda a: jnp.concatenate([a.reshape(1, -1), a.reshape(1, -1)], axis=1)
    w1p = jnp.zeros((hid, hid), F32).at[:w1.shape[0]].set(w1)
    mats = []
    for wmat in (_blockdiag2(w1p), _blockdiag2(w2), _blockdiag2(w3)):
        mats.append(_split(wmat))
    wst = jnp.concatenate([wfo, wfo], axis=0)
    wfh, wfl = _split(wst)
    args = (m1, two(fhalf), two(phalf), mats[0][0], mats[0][1], two(b1), mats[1][0], mats[1][1], two(b2),
            mats[2][0], mats[2][1], two(b3), two(fr), wfh, wfl, dec)
    assert len(args) == N_FILTER_REFS
    return args


def _filter_out(n2, C):
    shapes = (jax.ShapeDtypeStruct((DFT_N1, n2, C), jnp.int32), jax.ShapeDtypeStruct((1, C), F32))
    scratch = [pltpu.VMEM((n2 // 2, 2 * FILTER_HIDDEN), F32), pltpu.VMEM((n2 // 2, 2 * FILTER_HIDDEN), F32)]
    return shapes, scratch


def _filter_dft1(L, args):
    n2 = args[0].shape[1]
    C = args[-1].shape[1]
    full = lambda a: pl.BlockSpec(a.shape, lambda i: (0,) * a.ndim)
    nt = SUBLANES
    out_shape, scratch = _filter_out(n2, C)
    return pl.pallas_call(
        functools.partial(_filter_dft1_kernel, L),
        grid=(DFT_N1 // nt,), in_specs=[full(a) for a in args],
        out_specs=(pl.BlockSpec((nt, n2, C), lambda i: (i, 0, 0)), pl.BlockSpec((1, C), lambda i: (0, 0))),
        out_shape=out_shape, scratch_shapes=scratch,
        compiler_params=_params("arbitrary"),
        name="filter_dft1")(*args)


@functools.lru_cache(maxsize=None)
def _dft_tables(n1, n2):
    n = n1 * n2
    h = n2 // 2
    a2 = 2.0 * np.pi * np.outer(np.arange(n2), np.arange(n2)) / n2
    c2, s2 = np.cos(a2), np.sin(a2)
    m1d = np.empty((n2, 2, n2))
    m1d[:, 0, :h], m1d[:, 0, h:] = c2[:, :h], s2[:, :h]
    m1d[:, 1, :h], m1d[:, 1, h:] = -s2[:, :h], c2[:, :h]
    m1f = np.stack([c2, -s2], axis=1)
    a1 = 2.0 * np.pi * np.outer(np.arange(n1), np.arange(n1)) / n1
    f1r, f1i = np.cos(a1), -np.sin(a1)
    il = lambda m: m.reshape(m.shape[0], 2, -1).swapaxes(1, 2).reshape(m.shape[0], -1)
    pg = il(np.block([[f1r, -f1i], [f1i, f1r]]))
    pgs = il(np.block([[-f1i, -f1r], [f1r, -f1i]]))
    qd = il(np.block([[f1r, f1i], [-f1i, f1r]]).T).T
    at = 2.0 * np.pi * np.outer(np.arange(n2), np.arange(n1)) / n
    tr, ti = np.cos(at), -np.sin(at)
    txr = np.repeat(tr, 2, axis=1)
    txi = np.repeat(ti, 2, axis=1)
    er, ei = c2[:h] / n, s2[:h] / n
    pp = il(np.block([[er, -ei], [ei, er]]))
    pps = il(np.block([[-ei, -er], [er, -ei]]))
    ur, ui = np.cos(at).T, np.sin(at).T
    uxr = np.repeat(ur, 2, axis=1)
    uxi = np.repeat(ui, 2, axis=1)
    f32 = lambda a: np.asarray(a, np.float32)
    return dict(m1d=f32(m1d.reshape(2 * n2, n2)), m1f=f32(m1f.reshape(2 * n2, n2)),
                pg=f32(pg), pgs=f32(pgs), qd=f32(qd), txr=f32(txr), txi=f32(txi),
                pp=f32(pp), pps=f32(pps), uxr=f32(uxr), uxi=f32(uxi))


def _pack(x):
    return pltpu.bitcast(x.astype(BF16), jnp.int32)


def _unpack(w):
    return pltpu.bitcast(w, BF16)


def _dft1_kernel(m_ref, x_ref, out_ref):
    nt = out_ref.shape[0]
    m = m_ref[...]
    for j in range(nt):
        parts = [x_ref[b, :, j, :] for b in range(x_ref.shape[0])]
        x = parts[0] if len(parts) == 1 else jnp.concatenate(parts, axis=0)
        out_ref[j] = _pack(_dot(m, x.astype(BF16)))


def _dft1(m1, x):
    parts, rows, n1, C = x.shape
    nt = SUBLANES
    return pl.pallas_call(
        _dft1_kernel, grid=(n1 // nt,),
        in_specs=[pl.BlockSpec(m1.shape, lambda i: (0, 0)),
                  pl.BlockSpec((parts, rows, nt, C), lambda i: (0, 0, i, 0))],
        out_specs=pl.BlockSpec((nt, m1.shape[0] // 2, C), lambda i: (i, 0, 0)),
        out_shape=jax.ShapeDtypeStruct((n1, m1.shape[0] // 2, C), jnp.int32),
        compiler_params=_params("parallel"), name="dft_stage1")(m1, x)


def _conv_mid_kernel(a_ref, af_ref, l1_ref, pg_ref, pgs_ref, txr_ref, txi_ref, qd_ref, out_ref):
    n1 = a_ref.shape[0]
    qd = qd_ref[...]
    inv = 1.0 / l1_ref[...]
    for j in range(out_ref.shape[0]):
        gd = (pg_ref[...] * txr_ref[j:j + 1, :] + pgs_ref[...] * txi_ref[j:j + 1, :]).astype(BF16)
        x = _dot(gd, _unpack(a_ref[:, j, :]))
        kf = _dot(gd, _unpack(af_ref[:, j, :])) * inv
        xr, xi = x[:n1], x[n1:]
        kr, ki = kf[:n1], kf[n1:]
        y = jnp.concatenate([xr * kr - xi * ki, xr * ki + xi * kr], axis=0).astype(BF16)
        out_ref[j] = _pack(_dot(qd, y))


def _conv_mid(a, af, l1, tabs):
    n1, n2, C = a.shape
    kt = SUBLANES
    ablk = pl.BlockSpec((n1, kt, C), lambda i: (0, i, 0))
    mat = pl.BlockSpec((2 * n1, 2 * n1), lambda i: (0, 0))
    tw = pl.BlockSpec((kt, 2 * n1), lambda i: (i, 0))
    return pl.pallas_call(
        _conv_mid_kernel, grid=(n2 // kt,),
        in_specs=[ablk, ablk, pl.BlockSpec((1, C), lambda i: (0, 0)), mat, mat, tw, tw, mat],
        out_specs=pl.BlockSpec((kt, n1, C), lambda i: (i, 0, 0)),
        out_shape=jax.ShapeDtypeStruct((n2, n1, C), jnp.int32),
        compiler_params=_params("parallel"), name="conv_mid")(
            a, af, l1, tabs["pg"], tabs["pgs"], tabs["txr"], tabs["txi"], tabs["qd"])


def _conv_out_kernel(b_ref, pp_ref, pps_ref, uxr_ref, uxi_ref, out_ref):
    h = out_ref.shape[1]
    for j in range(out_ref.shape[2]):
        hm = (pp_ref[...] * uxr_ref[j:j + 1, :] + pps_ref[...] * uxi_ref[j:j + 1, :]).astype(BF16)
        y = _dot(hm, _unpack(b_ref[:, j, :]))
        out_ref[0, :, j, :] = y[:h]
        out_ref[1, :, j, :] = y[h:]


def _conv_out(bm, tabs):
    n2, n1, C = bm.shape
    nt = SUBLANES
    mat = pl.BlockSpec((n2, 2 * n2), lambda i: (0, 0))
    tw = pl.BlockSpec((nt, 2 * n2), lambda i: (i, 0))
    return pl.pallas_call(
        _conv_out_kernel, grid=(n1 // nt,),
        in_specs=[pl.BlockSpec((n2, nt, C), lambda i: (0, i, 0)), mat, mat, tw, tw],
        out_specs=pl.BlockSpec((2, n2 // 2, nt, C), lambda i: (0, 0, i, 0)),
        out_shape=jax.ShapeDtypeStruct((2, n2 // 2, n1, C), F32),
        compiler_params=_params("parallel"), name="conv_out")(
            bm, tabs["pp"], tabs["pps"], tabs["uxr"], tabs["uxi"])


def _long_conv(zc, af, l1):
    B, L, C = zc.shape
    assert B == 2, "the two batch rows are packed as one complex sequence"
    n1 = DFT_N1
    n2 = 2 * L // n1
    tabs = {k: jnp.asarray(v) for k, v in _dft_tables(n1, n2).items()}
    a = _dft1(tabs["m1d"].astype(BF16), zc.reshape(B, n2 // 2, n1, C))
    bm = _conv_mid(a, af, l1, tabs)
    return _conv_out(bm, tabs).reshape(B, L, C)


def _merge_kernel(x_ref, hf_ref, hb_ref, o_ref, y_ref, zc_ref, x0_ref, hn_ref, skip_ref, g_ref,
                  wga_ref, bga_ref, wgb_ref, bgb_ref, wa_ref, wb_ref, wo_ref, out_ref):
    x = x_ref[0]
    n = _rms(x, g_ref[...]).astype(BF16)
    ga = jax.nn.sigmoid(_dot(n, wga_ref[...]) + bga_ref[...])
    gb = jax.nn.sigmoid(_dot(n, wgb_ref[...]) + bgb_ref[...])
    heads = []
    for h in range(MLSTM_HEADS):
        hs = slice(h * HEAD_DIM, (h + 1) * HEAD_DIM)
        s = hf_ref[0, :, hs] + hb_ref[0, :, hs]
        dv = s - jnp.mean(s, axis=1, keepdims=True)
        var = jnp.mean(dv * dv, axis=1, keepdims=True)
        heads.append((dv * lax.rsqrt(var + EPS) * hn_ref[:, hs]
                      * jax.nn.sigmoid(o_ref[0, :, hs])).astype(BF16))
    ml = jnp.concatenate(heads, axis=1)
    hy = (x0_ref[0] * (y_ref[0] + skip_ref[...] * zc_ref[0])).astype(BF16)
    merged = ga * _dot(ml, wa_ref[...]) + gb * _dot(hy, wb_ref[...])
    out_ref[0] = x + _dot(merged.astype(BF16), wo_ref[...])


def _merge(x, hf, hb, o, y, zc, x0, hn, skip, g, wga, bga, wgb, bgb, wa, wb, wo):
    B, L, D = x.shape
    tm = min(TOKEN_BLOCK, L)
    full = lambda w: pl.BlockSpec(w.shape, lambda bb, i: (0,) * w.ndim)
    tok = lambda w: pl.BlockSpec((1, tm, w), lambda bb, i: (bb, i, 0))
    ws = (hn, skip, g, wga, bga, wgb, bgb, wa, wb, wo)
    acts = (hf, hb, o, y, zc, x0)
    return pl.pallas_call(
        _merge_kernel, grid=(B, L // tm),
        in_specs=[tok(D)] + [tok(a.shape[2]) for a in acts] + [full(w) for w in ws],
        out_specs=tok(D), out_shape=jax.ShapeDtypeStruct(x.shape, F32),
        compiler_params=_params("parallel", "parallel"), name="merge")(x, *acts, *ws)


def _mlp_kernel(final, nchunk, host_len, x_ref, g_ref, w1_ref, w2_ref, gf_ref, *rest):
    out_ref = rest[N_FILTER_REFS] if host_len else rest[0]
    if host_len:
        af_ref, l1_ref, cb_s, sb_s = rest[N_FILTER_REFS + 1:]
        step = pl.program_id(0) * pl.num_programs(1) + pl.program_id(1)

        @pl.when(step == 0)
        def _():
            _filter_init(host_len, rest, l1_ref, cb_s, sb_s)

        gs = af_ref.shape[0]

        def store(s, val):
            af_ref[s] = val
        hosted = _filter_pass(host_len, rest[:N_FILTER_REFS], step * gs, gs, store, l1_ref, cb_s, sb_s)
    else:
        hosted = iter(())

    x = x_ref[0]
    n = _rms(x, g_ref[...]).astype(BF16)
    fc = w1_ref.shape[1] // nchunk
    acc = x
    for c in range(nchunk):
        next(hosted, None)
        hcl = jnp.maximum(_dot(n, w1_ref[:, c * fc:(c + 1) * fc]), 0.0)
        acc = acc + _dot((hcl * hcl).astype(BF16), w2_ref[c * fc:(c + 1) * fc, :])
    for _ in hosted:
        pass
    out_ref[0] = _rms(acc, gf_ref[...]) if final else acc


def _mlp(x, g, w1, w2, gf, final, host=None):
    B, L, D = x.shape
    tm = min(TOKEN_BLOCK, L)
    steps = B * (L // tm)
    full = lambda w: pl.BlockSpec(w.shape, lambda bb, i: (0,) * w.ndim)
    once = lambda w: pl.BlockSpec(w.shape, lambda bb, i: (0,) * w.ndim, pipeline_mode=pl.Buffered(1))
    tok = pl.BlockSpec((1, tm, D), lambda bb, i: (bb, i, 0))
    args = [x, g, w1, w2, gf]
    in_specs = [tok, full(g), once(w1), once(w2), full(gf)]
    out_specs, out_shape, scratch = tok, jax.ShapeDtypeStruct(x.shape, F32), []
    sem = ("parallel", "parallel")
    host_len = 0
    if host is not None:
        host_len, fargs = host
        n2 = fargs[0].shape[1]
        C = fargs[-1].shape[1]
        gs = DFT_N1 // steps
        assert gs * steps == DFT_N1
        args += list(fargs)
        in_specs += [full(a) for a in fargs]
        per_l = L // tm
        fshape, scratch = _filter_out(n2, C)
        out_specs = (tok, pl.BlockSpec((gs, n2, C), lambda bb, i: (bb * per_l + i, 0, 0)),
                     pl.BlockSpec((1, C), lambda bb, i: (0, 0)))
        out_shape = (out_shape,) + fshape
        sem = ("arbitrary", "arbitrary")
    return pl.pallas_call(
        functools.partial(_mlp_kernel, final, 4, host_len), grid=(B, L // tm),
        in_specs=in_specs, out_specs=out_specs, out_shape=out_shape, scratch_shapes=scratch,
        compiler_params=_params(*sem), name="mlp")(*args)


def _layer(x, p, l, filt, host):
    depth = p["w_in"].shape[0]
    D = x.shape[2]
    dm = HEAD_DIM * MLSTM_HEADS
    ng = 4 * MLSTM_HEADS
    dh = p["hy_skip"].shape[1]
    o_g, o_u, o_ga = 4 * dm, 4 * dm + ng, 4 * dm + ng + 3 * dh
    row = lambda a: a.reshape(1, -1)
    col = lambda a: a.reshape(-1, 1)
    w_in = p["w_in"][l]
    b_in = p["b_in"][l]
    w_qvo = jnp.concatenate([w_in[:, :dm], w_in[:, 2 * dm:4 * dm]], axis=1)
    b_qvo = jnp.concatenate([b_in[:dm], b_in[2 * dm:4 * dm]])
    q, kt, v, o, g_col, g_row, x0, zc = _inproj(
        x, row(p["norm_mix"][l]), w_qvo.astype(BF16), row(b_qvo),
        w_in[:, dm:2 * dm].T.astype(BF16), col(b_in[dm:2 * dm]),
        w_in[:, o_g:o_u].T.astype(BF16), col(b_in[o_g:o_u]),
        w_in[:, o_u:o_ga].astype(BF16), row(b_in[o_u:o_ga]),
        p["hy_conv_w"][l], row(p["hy_conv_b"][l]))
    h_fwd, h_bwd = _mlstm(q, kt, v, g_col, g_row)
    y = _long_conv(zc, *filt)
    x = _merge(x, h_fwd, h_bwd, o, y, zc, x0, row(p["mlstm_norm"][l]), row(p["hy_skip"][l]),
               row(p["norm_mix"][l]),
               w_in[:, o_ga:o_ga + D].astype(BF16), row(b_in[o_ga:o_ga + D]),
               w_in[:, o_ga + D:].astype(BF16), row(b_in[o_ga + D:]),
               p["w_branch_a"][l].astype(BF16), p["w_branch_b"][l].astype(BF16),
               p["w_out"][l].astype(BF16))
    out = _mlp(x, row(p["norm_mlp"][l]), p["w_mlp1"][l].astype(BF16), p["w_mlp2"][l].astype(BF16),
               row(p["norm_final"]), l == depth - 1, host)
    return (out[0], out[1:]) if host is not None else (out, None)


def _filter_operands(p, l, L):
    n2 = 2 * L // DFT_N1
    m1f = jnp.asarray(_dft_tables(DFT_N1, n2)["m1f"]).astype(BF16)
    return _filter_args(m1f, p["hy_w1"][l], p["hy_b1"][l], p["hy_w2"][l], p["hy_b2"][l], p["hy_w3"][l],
                        p["hy_b3"][l], p["hy_freq"][l], p["hy_w_fo"][l], p["hy_decay"][l])


def _run(xs, p):
    depth = p["w_in"].shape[0]
    order = sorted(range(len(xs)), key=lambda gi: xs[gi].shape[1])
    blocks = [(gi, l) for l in range(depth) for gi in order]
    xs = list(xs)
    g0, l0 = blocks[0]
    filt = _filter_dft1(xs[g0].shape[1], _filter_operands(p, l0, xs[g0].shape[1]))
    for bi, (gi, l) in enumerate(blocks):
        host = None
        if bi + 1 < len(blocks):
            gn, ln = blocks[bi + 1]
            host = (xs[gn].shape[1], _filter_operands(p, ln, xs[gn].shape[1]))
        xs[gi], filt = _layer(xs[gi], p, l, filt, host)
    return tuple(xs)


def kernel(x_prompt, x_sample, norm_mix, w_in, b_in, mlstm_norm, hy_conv_w, hy_conv_b, hy_w1, hy_b1, hy_w2, hy_b2, hy_w3, hy_b3, hy_freq, hy_w_fo, hy_decay, hy_skip, w_branch_a, w_branch_b, w_out, norm_mlp, w_mlp1, w_mlp2, norm_final):
    p = dict(norm_mix=norm_mix, w_in=w_in, b_in=b_in, mlstm_norm=mlstm_norm, hy_conv_w=hy_conv_w,
             hy_conv_b=hy_conv_b, hy_w1=hy_w1, hy_b1=hy_b1, hy_w2=hy_w2, hy_b2=hy_b2, hy_w3=hy_w3,
             hy_b3=hy_b3, hy_freq=hy_freq, hy_w_fo=hy_w_fo, hy_decay=hy_decay, hy_skip=hy_skip,
             w_branch_a=w_branch_a, w_branch_b=w_branch_b, w_out=w_out, norm_mlp=norm_mlp,
             w_mlp1=w_mlp1, w_mlp2=w_mlp2, norm_final=norm_final)
    return _run((x_prompt, x_sample), p)
```

```python
import functools
import math

import numpy as np
import jax
import jax.numpy as jnp
from jax import lax
from jax.experimental import pallas as pl
from jax.experimental.pallas import tpu as pltpu

F32 = jnp.float32
BF16 = jnp.bfloat16

EPS = 1e-6
MLSTM_HEADS = 4
HEAD_DIM = 128
CHUNK = 256
HYENA_BANDS = 16
FILTER_HIDDEN = 64
DFT_N1 = 128
SUBLANES = 8
TOKEN_BLOCK = 512
VMEM_LIMIT = 48 * 1024 * 1024
HIGHEST = lax.Precision.HIGHEST


def _params(*sem):
    return pltpu.CompilerParams(dimension_semantics=sem, vmem_limit_bytes=VMEM_LIMIT)


def _rms(x, g):
    return x * lax.rsqrt(jnp.mean(x * x, axis=-1, keepdims=True) + EPS) * g


def _dot(a, b):
    return jnp.dot(a, b, preferred_element_type=F32)


def _dot_nt(a, b):
    return lax.dot_general(a, b, (((1,), (1,)), ((), ())), preferred_element_type=F32)


def _split(a):
    hi = a.astype(BF16)
    return hi, (a - hi.astype(F32)).astype(BF16)


def _dot3(a, b_hi, b_lo):
    a_hi, a_lo = _split(a)
    return _dot(a_hi, b_hi) + (_dot(a_hi, b_lo) + _dot(a_lo, b_hi))


def _inproj_kernel(host_len, x_ref, xp_ref, xn_ref, g_ref, wm_ref, bm_ref, wkt_ref, bk_ref,
                   wgt_ref, bgt_ref, wu_ref, bu_ref, cw_ref, cb_ref, *rest):
    if host_len:
        (q_ref, kt_ref, v_ref, o_ref, gc_ref, gr_ref, x0_ref, zc_ref,
         af_ref, l1_ref, cb_s, sb_s) = rest[N_FILTER_REFS:]
        step = pl.program_id(0) * pl.num_programs(1) + pl.program_id(1)

        @pl.when(step == 0)
        def _():
            _filter_init(host_len, rest, l1_ref, cb_s, sb_s)

        gs = af_ref.shape[0]

        def store(s, val):
            af_ref[s] = val
        hosted = _filter_pass(host_len, rest[:N_FILTER_REFS], step * gs, gs, store, l1_ref, cb_s, sb_s)
    else:
        q_ref, kt_ref, v_ref, o_ref, gc_ref, gr_ref, x0_ref, zc_ref = rest
        hosted = iter(())
    i = pl.program_id(1)
    last = pl.num_programs(1) - 1
    g = g_ref[...]
    n = _rms(x_ref[0], g).astype(BF16)
    dm = HEAD_DIM * MLSTM_HEADS
    next(hosted, None)
    pm = _dot(n, wm_ref[...]) + bm_ref[...]
    q_ref[0] = (pm[:, :dm] * (HEAD_DIM ** -0.5)).astype(BF16)
    v_ref[0] = pm[:, dm:2 * dm].astype(BF16)
    o_ref[0] = pm[:, 2 * dm:]
    next(hosted, None)
    kt_ref[0] = (_dot_nt(wkt_ref[...], n) + bk_ref[...]).astype(BF16)
    gates = _dot_nt(wgt_ref[...], n) + bgt_ref[...]
    gr_ref[0] = gates
    gc_ref[0] = gates.T
    next(hosted, None)

    tm = n.shape[0]
    halo = _rms(jnp.concatenate([xp_ref[0], xn_ref[0]], axis=0), g).astype(BF16)
    u_all = _dot(jnp.concatenate([n, halo], axis=0), wu_ref[...]) + bu_ref[...]
    u = u_all[:tm]
    u_prev = jnp.where(i > 0, u_all[tm + SUBLANES - 1:tm + SUBLANES], 0.0)
    u_next = jnp.where(i < last, u_all[tm + SUBLANES:tm + SUBLANES + 1], 0.0)
    row = lax.broadcasted_iota(jnp.int32, (tm, 1), 0)
    u_m1 = jnp.where(row == 0, u_prev, pltpu.roll(u, 1, 0))
    u_p1 = jnp.where(row == tm - 1, u_next, pltpu.roll(u, tm - 1, 0))
    cw = cw_ref[...]
    c = u_m1 * cw[0:1] + u * cw[1:2] + u_p1 * cw[2:3] + cb_ref[...]
    dh = c.shape[1] // 3
    x0_ref[0] = c[:, :dh]
    zc_ref[0] = c[:, dh:2 * dh] * c[:, 2 * dh:]
    for _ in hosted:
        pass


def _inproj(x, g, wm, bm, wkt, bk, wgt, bgt, wu, bu, cw, cb, host=None):
    B, L, D = x.shape
    tm = min(TOKEN_BLOCK, L)
    nb = L // tm
    r8 = tm // SUBLANES
    dm = wkt.shape[0]
    dh = wu.shape[1] // 3
    ng = wgt.shape[0]
    ws = (g, wm, bm, wkt, bk, wgt, bgt, wu, bu, cw, cb)
    full = lambda a: pl.BlockSpec(a.shape, lambda b, i: (0,) * a.ndim)
    tok = lambda w: pl.BlockSpec((1, tm, w), lambda b, i: (b, i, 0))
    tr = lambda w: pl.BlockSpec((1, w, tm), lambda b, i: (b, 0, i))
    in_specs = [
        tok(D),
        pl.BlockSpec((1, SUBLANES, D), lambda b, i: (b, jnp.maximum(i * r8 - 1, 0), 0)),
        pl.BlockSpec((1, SUBLANES, D), lambda b, i: (b, jnp.minimum((i + 1) * r8, L // SUBLANES - 1), 0)),
    ] + [full(w) for w in ws]
    sds = jax.ShapeDtypeStruct
    out_shape = (sds((B, L, dm), BF16), sds((B, dm, L), BF16), sds((B, L, dm), BF16), sds((B, L, dm), F32),
                 sds((B, L, ng), F32), sds((B, ng, L), F32), sds((B, L, dh), F32), sds((B, L, dh), F32))
    out_specs = (tok(dm), tr(dm), tok(dm), tok(dm), tok(ng), tr(ng), tok(dh), tok(dh))
    sem, scratch, host_len, fargs = ("parallel", "parallel"), [], 0, ()
    if host is not None:
        host_len, fargs = host
        n2, C = fargs[0].shape[1], fargs[-1].shape[1]
        gs = DFT_N1 // (B * nb)
        assert gs * B * nb == DFT_N1
        in_specs += [full(a) for a in fargs]
        fshape, scratch = _filter_out(n2, C)
        out_specs += (pl.BlockSpec((gs, n2, C), lambda b, i: (b * nb + i, 0, 0)),
                      pl.BlockSpec((1, C), lambda b, i: (0, 0)))
        out_shape += fshape
        sem = ("arbitrary", "arbitrary")
    return pl.pallas_call(
        functools.partial(_inproj_kernel, host_len), grid=(B, nb), in_specs=in_specs,
        out_specs=out_specs, out_shape=out_shape, scratch_shapes=scratch,
        compiler_params=_params(*sem), name="inproj")(x, x, x, *ws, *fargs)


def _log_sigmoid(x):
    return jnp.minimum(x, 0.0) - jnp.log1p(jnp.exp(-jnp.abs(x)))


def _mlstm_dir(reverse, q_ref, kt_ref, v_ref, gc_ref, gr_ref, out_ref, c_s, m_s, slot):
    nh = MLSTM_HEADS
    nb = q_ref.shape[0]
    nq = nb * nh
    ch = q_ref.shape[1]
    off = 2 * nh if reverse else 0
    ri = lax.broadcasted_iota(jnp.int32, (ch, ch), 0)
    ci = lax.broadcasted_iota(jnp.int32, (ch, ch), 1)
    mask = (ri <= ci) if reverse else (ci <= ri)
    tri_row = ((ci <= ri) if reverse else (ri <= ci)).astype(F32)

    li_col = jnp.concatenate([gc_ref[b][:, off:off + nh] for b in range(nb)], axis=1)
    li_row = jnp.concatenate([gr_ref[b][off:off + nh, :] for b in range(nb)], axis=0)
    lf_row = _log_sigmoid(jnp.concatenate([gr_ref[b][off + nh:off + 2 * nh, :] for b in range(nb)], axis=0))
    b_row = jnp.dot(lf_row, tri_row, precision=HIGHEST, preferred_element_type=F32)
    b_col = b_row.T
    beta_col = li_col - b_col
    beta_row = li_row - b_row
    cmx = beta_col
    rowi = lax.broadcasted_iota(jnp.int32, (ch, nq), 0)
    k = 1
    while k < ch:
        if reverse:
            cmx = jnp.maximum(cmx, jnp.where(rowi < ch - k, pltpu.roll(cmx, ch - k, 0), -jnp.inf))
        else:
            cmx = jnp.maximum(cmx, jnp.where(rowi >= k, pltpu.roll(cmx, k, 0), -jnp.inf))
        k *= 2
    last = 0 if reverse else ch - 1
    g_row = b_col[last:last + 1, :]
    m_loc = g_row + cmx[last:last + 1, :]
    m_prev = m_s[slot:slot + 1, 0:nq]
    mx = jnp.maximum(m_prev, cmx)
    s_inter = jnp.exp(m_prev - mx)
    inv_floor = jnp.exp(-mx - b_col)
    m_new = jnp.maximum(g_row + m_prev, m_loc)
    s_old = jnp.exp(g_row + m_prev - m_new)
    s_new = jnp.exp(m_loc - m_new)
    w_row = jnp.exp(beta_row - jnp.max(beta_row, axis=1, keepdims=True))
    m_s[slot:slot + 1, 0:nq] = m_new

    ones_col = (lax.broadcasted_iota(jnp.int32, (ch, HEAD_DIM), 1) == 0).astype(BF16)
    for b in range(nb):
        for h in range(nh):
            i = b * nh + h
            hs = slice(h * HEAD_DIM, (h + 1) * HEAD_DIM)
            q = q_ref[b, :, hs]
            kt = kt_ref[b, hs, :]
            v_aug = jnp.concatenate([v_ref[b, :, hs], ones_col], axis=1)
            d_mat = jnp.exp(jnp.where(mask, beta_row[i:i + 1, :] - mx[:, i:i + 1], -jnp.inf))
            scores = (_dot(q, kt) * d_mat).astype(BF16)
            c_prev = c_s[slot * nq + i]
            num = _dot(scores, v_aug) + s_inter[:, i:i + 1] * _dot(q, c_prev.astype(BF16))
            den = num[:, HEAD_DIM:HEAD_DIM + 1]
            out_ref[b, :, hs] = num[:, :HEAD_DIM] * (1.0 / jnp.maximum(jnp.abs(den), inv_floor[:, i:i + 1]))
            ktw = (kt.astype(F32) * w_row[i:i + 1, :]).astype(BF16)
            c_s[slot * nq + i] = s_old[:, i:i + 1] * c_prev + s_new[:, i:i + 1] * _dot(ktw, v_aug)


def _mlstm_kernel(qf, ktf, vf, gcf, grf, qb, ktb, vb, gcb, grb, of, ob, c_s, m_s):
    @pl.when(pl.program_id(0) == 0)
    def _():
        c_s[...] = jnp.zeros_like(c_s)
        m_s[...] = jnp.zeros_like(m_s)

    _mlstm_dir(False, qf, ktf, vf, gcf, grf, of, c_s, m_s, 0)
    _mlstm_dir(True, qb, ktb, vb, gcb, grb, ob, c_s, m_s, 1)


def _mlstm(q, kt, v, g_col, g_row):
    B, L, dm = q.shape
    ch = min(CHUNK, L)
    nc = L // ch
    specs = []
    for cidx in ((lambda c: c), (lambda c: nc - 1 - c)):
        tok = lambda w, cidx=cidx: pl.BlockSpec((B, ch, w), lambda c: (0, cidx(c), 0))
        tr = lambda w, cidx=cidx: pl.BlockSpec((B, w, ch), lambda c: (0, 0, cidx(c)))
        specs.append(([tok(dm), tr(dm), tok(dm), tok(g_col.shape[2]), tr(g_row.shape[1])], tok(dm)))
    args = [q, kt, v, g_col, g_row]
    out = jax.ShapeDtypeStruct((B, L, dm), F32)
    return pl.pallas_call(
        _mlstm_kernel, grid=(nc,), in_specs=specs[0][0] + specs[1][0],
        out_specs=(specs[0][1], specs[1][1]), out_shape=(out, out),
        scratch_shapes=[pltpu.VMEM((2 * B * MLSTM_HEADS, HEAD_DIM, 2 * HEAD_DIM), F32),
                        pltpu.VMEM((SUBLANES, 128), F32)],
        compiler_params=_params("arbitrary"), name="mlstm")(*args, *args)


N_FILTER_REFS = 16


def _filter_init(seq_len, refs, l1_ref, cb_s, sb_s):
    half = refs[0].shape[1] // 2
    hid = FILTER_HIDDEN
    grp = lax.broadcasted_iota(jnp.int32, (half, 2 * hid), 1) >= hid
    r2 = lax.broadcasted_iota(jnp.int32, (half, 2 * hid), 0)
    l1_ref[...] = jnp.zeros_like(l1_ref)
    lag0 = DFT_N1 * jnp.where(grp, half - r2, r2)
    ang = refs[1][...] * (lag0.astype(F32) * ((2.0 * math.pi) / seq_len))
    cb_s[...] = jnp.cos(ang)
    sb_s[...] = jnp.sin(ang)


def _filter_pass(seq_len, refs, n1_first, gs, store, l1_ref, cb_s, sb_s):
    (m_ref, fvec_ref, ph_ref, w1h_ref, w1l_ref, b1_ref, w2h_ref, w2l_ref, b2_ref, w3h_ref, w3l_ref,
     b3_ref, fr_ref, wfh_ref, wfl_ref, dec_ref) = refs
    half = m_ref.shape[1] // 2
    hid = FILTER_HIDDEN
    C = dec_ref.shape[1]
    L = seq_len
    lane = lax.broadcasted_iota(jnp.int32, (half, 2 * hid), 1)
    grp = lane >= hid
    feat = lane & (hid - 1)
    r2 = lax.broadcasted_iota(jnp.int32, (half, 2 * hid), 0)
    r_col = lax.broadcasted_iota(jnp.int32, (half, 1), 0)
    fvec = fvec_ref[...]
    fr = fr_ref[...]
    sign = jnp.where(grp[0:1], -1.0, 1.0)
    scale = (2.0 * math.pi) / L

    def lag(n):
        return jnp.where(n < L, n, jnp.where(n == L, 0, 2 * L - n)).astype(F32)

    grp_s = jnp.concatenate([grp] * gs, axis=0)
    zs = []
    for s in range(gs):
        n1 = n1_first + s
        n = n1 + DFT_N1 * (r2 + jnp.where(grp, half, 0))
        ang1 = sign * (fvec * (n1.astype(F32) * scale)) + ph_ref[...]
        feats = jnp.cos(ang1) * cb_s[...] - jnp.sin(ang1) * sb_s[...]
        feats = jnp.where(n == L, jnp.cos(ph_ref[...]), feats)
        zs.append(jnp.where(feat == 0, lag(n) / (L - 1),
                            jnp.where(feat <= 2 * HYENA_BANDS, feats, 0.0)))
    z = jnp.concatenate(zs, axis=0)
    h = jnp.sin(fr * (_dot3(z, w1h_ref[...], w1l_ref[...]) + b1_ref[...]))
    yield
    h = jnp.sin(fr * (_dot3(h, w2h_ref[...], w2l_ref[...]) + b2_ref[...]))
    yield
    h = jnp.sin(fr * (_dot3(h, w3h_ref[...], w3l_ref[...]) + b3_ref[...]))
    yield
    taps = []
    for gi in range(2):
        cs = slice(gi * C, (gi + 1) * C)
        hg = jnp.where(grp_s == (gi == 1), h, 0.0)
        tp = _dot3(hg, wfh_ref[:, cs], wfl_ref[:, cs])
        dec = jnp.abs(dec_ref[gi:gi + 1, :])
        parts = []
        for s in range(gs):
            n_col = n1_first + s + DFT_N1 * (r_col + gi * half)
            tps = tp[s * half:(s + 1) * half] * jnp.exp(-(lag(n_col) / (L - 1)) * dec)
            l1_ref[...] += jnp.sum(jnp.abs(tps), axis=0, keepdims=True)
            parts.append(jnp.where(n_col == L, 0.0, tps).astype(BF16))
        taps.append(parts)
    for s in range(gs):
        store(s, _pack(_dot(m_ref[...], jnp.concatenate([taps[0][s], taps[1][s]], axis=0))))


def _blockdiag2(w):
    z = jnp.zeros_like(w)
    return jnp.concatenate([jnp.concatenate([w, z], axis=1), jnp.concatenate([z, w], axis=1)], axis=0)


def _filter_args(m1, w1, b1, w2, b2, w3, b3, fr, wfo, dec):
    hid = FILTER_HIDDEN
    fgrid = jnp.linspace(1e-4, HYENA_BANDS - 1, HYENA_BANDS, dtype=F32)
    fhalf = jnp.zeros((hid,), F32).at[1:1 + HYENA_BANDS].set(fgrid).at[1 + HYENA_BANDS:1 + 2 * HYENA_BANDS].set(fgrid)
    phalf = jnp.zeros((hid,), F32).at[1 + HYENA_BANDS:1 + 2 * HYENA_BANDS].set(0.5 * math.pi)
    two = lambda a: jnp.concatenate([a.reshape(1, -1), a.reshape(1, -1)], axis=1)
    w1p = jnp.zeros((hid, hid), F32).at[:w1.shape[0]].set(w1)
    mats = []
    for wmat in (_blockdiag2(w1p), _blockdiag2(w2), _blockdiag2(w3)):
        mats.append(_split(wmat))
    wst = jnp.concatenate([wfo, wfo], axis=0)
    wfh, wfl = _split(wst)
    args = (m1, two(fhalf), two(phalf), mats[0][0], mats[0][1], two(b1), mats[1][0], mats[1][1], two(b2),
            mats[2][0], mats[2][1], two(b3), two(fr), wfh, wfl, dec)
    assert len(args) == N_FILTER_REFS
    return args


def _filter_out(n2, C):
    shapes = (jax.ShapeDtypeStruct((DFT_N1, n2, C), jnp.int32), jax.ShapeDtypeStruct((1, C), F32))
    scratch = [pltpu.VMEM((n2 // 2, 2 * FILTER_HIDDEN), F32), pltpu.VMEM((n2 // 2, 2 * FILTER_HIDDEN), F32)]
    return shapes, scratch


@functools.lru_cache(maxsize=None)
def _dft_tables(n1, n2):
    n = n1 * n2
    h = n2 // 2
    a2 = 2.0 * np.pi * np.outer(np.arange(n2), np.arange(n2)) / n2
    c2, s2 = np.cos(a2), np.sin(a2)
    m1d = np.empty((n2, 2, n2))
    m1d[:, 0, :h], m1d[:, 0, h:] = c2[:, :h], s2[:, :h]
    m1d[:, 1, :h], m1d[:, 1, h:] = -s2[:, :h], c2[:, :h]
    m1f = np.stack([c2, -s2], axis=1)
    a1 = 2.0 * np.pi * np.outer(np.arange(n1), np.arange(n1)) / n1
    f1r, f1i = np.cos(a1), -np.sin(a1)
    il = lambda m: m.reshape(m.shape[0], 2, -1).swapaxes(1, 2).reshape(m.shape[0], -1)
    pg = il(np.block([[f1r, -f1i], [f1i, f1r]]))
    pgs = il(np.block([[-f1i, -f1r], [f1r, -f1i]]))
    qd = il(np.block([[f1r, f1i], [-f1i, f1r]]).T).T
    at = 2.0 * np.pi * np.outer(np.arange(n2), np.arange(n1)) / n
    tr, ti = np.cos(at), -np.sin(at)
    txr = np.repeat(tr, 2, axis=1)
    txi = np.repeat(ti, 2, axis=1)
    er, ei = c2[:h] / n, s2[:h] / n
    pp = il(np.block([[er, -ei], [ei, er]]))
    pps = il(np.block([[-ei, -er], [er, -ei]]))
    ur, ui = np.cos(at).T, np.sin(at).T
    uxr = np.repeat(ur, 2, axis=1)
    uxi = np.repeat(ui, 2, axis=1)
    f32 = lambda a: np.asarray(a, np.float32)
    return dict(m1d=f32(m1d.reshape(2 * n2, n2)), m1f=f32(m1f.reshape(2 * n2, n2)),
                pg=f32(pg), pgs=f32(pgs), qd=f32(qd), txr=f32(txr), txi=f32(txi),
                pp=f32(pp), pps=f32(pps), uxr=f32(uxr), uxi=f32(uxi))


def _pack(x):
    return pltpu.bitcast(x.astype(BF16), jnp.int32)


def _unpack(w):
    return pltpu.bitcast(w, BF16)


def _dft1_kernel(m_ref, x_ref, out_ref):
    nt = out_ref.shape[0]
    m = m_ref[...]
    for j in range(nt):
        parts = [x_ref[b, :, j, :] for b in range(x_ref.shape[0])]
        x = parts[0] if len(parts) == 1 else jnp.concatenate(parts, axis=0)
        out_ref[j] = _pack(_dot(m, x.astype(BF16)))


def _dft1(m1, x):
    parts, rows, n1, C = x.shape
    nt = SUBLANES
    return pl.pallas_call(
        _dft1_kernel, grid=(n1 // nt,),
        in_specs=[pl.BlockSpec(m1.shape, lambda i: (0, 0)),
                  pl.BlockSpec((parts, rows, nt, C), lambda i: (0, 0, i, 0))],
        out_specs=pl.BlockSpec((nt, m1.shape[0] // 2, C), lambda i: (i, 0, 0)),
        out_shape=jax.ShapeDtypeStruct((n1, m1.shape[0] // 2, C), jnp.int32),
        compiler_params=_params("parallel"), name="dft_stage1")(m1, x)


def _conv_mid_kernel(a_ref, af_ref, l1_ref, pg_ref, pgs_ref, txr_ref, txi_ref, qd_ref, out_ref):
    n1 = a_ref.shape[0]
    qd = qd_ref[...]
    inv = 1.0 / l1_ref[...]
    for j in range(out_ref.shape[0]):
        gd = (pg_ref[...] * txr_ref[j:j + 1, :] + pgs_ref[...] * txi_ref[j:j + 1, :]).astype(BF16)
        x = _dot(gd, _unpack(a_ref[:, j, :]))
        kf = _dot(gd, _unpack(af_ref[:, j, :])) * inv
        xr, xi = x[:n1], x[n1:]
        kr, ki = kf[:n1], kf[n1:]
        y = jnp.concatenate([xr * kr - xi * ki, xr * ki + xi * kr], axis=0).astype(BF16)
        out_ref[j] = _pack(_dot(qd, y))


def _conv_mid(a, af, l1, tabs):
    n1, n2, C = a.shape
    kt = SUBLANES
    ablk = pl.BlockSpec((n1, kt, C), lambda i: (0, i, 0))
    mat = pl.BlockSpec((2 * n1, 2 * n1), lambda i: (0, 0))
    tw = pl.BlockSpec((kt, 2 * n1), lambda i: (i, 0))
    return pl.pallas_call(
        _conv_mid_kernel, grid=(n2 // kt,),
        in_specs=[ablk, ablk, pl.BlockSpec((1, C), lambda i: (0, 0)), mat, mat, tw, tw, mat],
        out_specs=pl.BlockSpec((kt, n1, C), lambda i: (i, 0, 0)),
        out_shape=jax.ShapeDtypeStruct((n2, n1, C), jnp.int32),
        compiler_params=_params("parallel"), name="conv_mid")(
            a, af, l1, tabs["pg"], tabs["pgs"], tabs["txr"], tabs["txi"], tabs["qd"])


def _conv_out_kernel(b_ref, pp_ref, pps_ref, uxr_ref, uxi_ref, out_ref):
    h = out_ref.shape[1]
    for j in range(out_ref.shape[2]):
        hm = (pp_ref[...] * uxr_ref[j:j + 1, :] + pps_ref[...] * uxi_ref[j:j + 1, :]).astype(BF16)
        y = _dot(hm, _unpack(b_ref[:, j, :]))
        out_ref[0, :, j, :] = y[:h]
        out_ref[1, :, j, :] = y[h:]


def _conv_out(bm, tabs):
    n2, n1, C = bm.shape
    nt = SUBLANES
    mat = pl.BlockSpec((n2, 2 * n2), lambda i: (0, 0))
    tw = pl.BlockSpec((nt, 2 * n2), lambda i: (i, 0))
    return pl.pallas_call(
        _conv_out_kernel, grid=(n1 // nt,),
        in_specs=[pl.BlockSpec((n2, nt, C), lambda i: (0, i, 0)), mat, mat, tw, tw],
        out_specs=pl.BlockSpec((2, n2 // 2, nt, C), lambda i: (0, 0, i, 0)),
        out_shape=jax.ShapeDtypeStruct((2, n2 // 2, n1, C), F32),
        compiler_params=_params("parallel"), name="conv_out")(
            bm, tabs["pp"], tabs["pps"], tabs["uxr"], tabs["uxi"])


def _long_conv(zc, af, l1):
    B, L, C = zc.shape
    assert B == 2, "the two batch rows are packed as one complex sequence"
    n1 = DFT_N1
    n2 = 2 * L // n1
    tabs = {k: jnp.asarray(v) for k, v in _dft_tables(n1, n2).items()}
    a = _dft1(tabs["m1d"].astype(BF16), zc.reshape(B, n2 // 2, n1, C))
    bm = _conv_mid(a, af, l1, tabs)
    return _conv_out(bm, tabs).reshape(B, L, C)


def _merge_kernel(x_ref, hf_ref, hb_ref, o_ref, y_ref, zc_ref, x0_ref, hn_ref, skip_ref, g_ref,
                  wga_ref, bga_ref, wgb_ref, bgb_ref, wa_ref, wb_ref, wo_ref, out_ref):
    x = x_ref[0]
    n = _rms(x, g_ref[...]).astype(BF16)
    ga = jax.nn.sigmoid(_dot(n, wga_ref[...]) + bga_ref[...])
    gb = jax.nn.sigmoid(_dot(n, wgb_ref[...]) + bgb_ref[...])
    heads = []
    for h in range(MLSTM_HEADS):
        hs = slice(h * HEAD_DIM, (h + 1) * HEAD_DIM)
        s = hf_ref[0, :, hs] + hb_ref[0, :, hs]
        dv = s - jnp.mean(s, axis=1, keepdims=True)
        var = jnp.mean(dv * dv, axis=1, keepdims=True)
        heads.append((dv * lax.rsqrt(var + EPS) * hn_ref[:, hs]
                      * jax.nn.sigmoid(o_ref[0, :, hs])).astype(BF16))
    ml = jnp.concatenate(heads, axis=1)
    hy = (x0_ref[0] * (y_ref[0] + skip_ref[...] * zc_ref[0])).astype(BF16)
    merged = ga * _dot(ml, wa_ref[...]) + gb * _dot(hy, wb_ref[...])
    out_ref[0] = x + _dot(merged.astype(BF16), wo_ref[...])


def _merge(x, hf, hb, o, y, zc, x0, hn, skip, g, wga, bga, wgb, bgb, wa, wb, wo):
    B, L, D = x.shape
    tm = min(TOKEN_BLOCK, L)
    full = lambda w: pl.BlockSpec(w.shape, lambda bb, i: (0,) * w.ndim)
    tok = lambda w: pl.BlockSpec((1, tm, w), lambda bb, i: (bb, i, 0))
    ws = (hn, skip, g, wga, bga, wgb, bgb, wa, wb, wo)
    acts = (hf, hb, o, y, zc, x0)
    return pl.pallas_call(
        _merge_kernel, grid=(B, L // tm),
        in_specs=[tok(D)] + [tok(a.shape[2]) for a in acts] + [full(w) for w in ws],
        out_specs=tok(D), out_shape=jax.ShapeDtypeStruct(x.shape, F32),
        compiler_params=_params("parallel", "parallel"), name="merge")(x, *acts, *ws)


def _mlp_kernel(final, nchunk, host_len, x_ref, g_ref, w1_ref, w2_ref, gf_ref, *rest):
    out_ref = rest[N_FILTER_REFS] if host_len else rest[0]
    if host_len:
        af_ref, l1_ref, cb_s, sb_s = rest[N_FILTER_REFS + 1:]
        step = pl.program_id(0) * pl.num_programs(1) + pl.program_id(1)

        @pl.when(step == 0)
        def _():
            _filter_init(host_len, rest, l1_ref, cb_s, sb_s)

        gs = af_ref.shape[0]

        def store(s, val):
            af_ref[s] = val
        hosted = _filter_pass(host_len, rest[:N_FILTER_REFS], step * gs, gs, store, l1_ref, cb_s, sb_s)
    else:
        hosted = iter(())

    x = x_ref[0]
    n = _rms(x, g_ref[...]).astype(BF16)
    fc = w1_ref.shape[1] // nchunk
    acc = x
    for c in range(nchunk):
        next(hosted, None)
        hcl = jnp.maximum(_dot(n, w1_ref[:, c * fc:(c + 1) * fc]), 0.0)
        acc = acc + _dot((hcl * hcl).astype(BF16), w2_ref[c * fc:(c + 1) * fc, :])
    for _ in hosted:
        pass
    out_ref[0] = _rms(acc, gf_ref[...]) if final else acc


def _mlp(x, g, w1, w2, gf, final, host=None):
    B, L, D = x.shape
    tm = min(TOKEN_BLOCK, L)
    steps = B * (L // tm)
    full = lambda w: pl.BlockSpec(w.shape, lambda bb, i: (0,) * w.ndim)
    once = lambda w: pl.BlockSpec(w.shape, lambda bb, i: (0,) * w.ndim, pipeline_mode=pl.Buffered(1))
    tok = pl.BlockSpec((1, tm, D), lambda bb, i: (bb, i, 0))
    args = [x, g, w1, w2, gf]
    in_specs = [tok, full(g), once(w1), once(w2), full(gf)]
    out_specs, out_shape, scratch = tok, jax.ShapeDtypeStruct(x.shape, F32), []
    sem = ("parallel", "parallel")
    host_len = 0
    if host is not None:
        host_len, fargs = host
        n2 = fargs[0].shape[1]
        C = fargs[-1].shape[1]
        gs = DFT_N1 // steps
        assert gs * steps == DFT_N1
        args += list(fargs)
        in_specs += [full(a) for a in fargs]
        per_l = L // tm
        fshape, scratch = _filter_out(n2, C)
        out_specs = (tok, pl.BlockSpec((gs, n2, C), lambda bb, i: (bb * per_l + i, 0, 0)),
                     pl.BlockSpec((1, C), lambda bb, i: (0, 0)))
        out_shape = (out_shape,) + fshape
        sem = ("arbitrary", "arbitrary")
    return pl.pallas_call(
        functools.partial(_mlp_kernel, final, 4, host_len), grid=(B, L // tm),
        in_specs=in_specs, out_specs=out_specs, out_shape=out_shape, scratch_shapes=scratch,
        compiler_params=_params(*sem), name="mlp")(*args)


def _layer(x, p, l, filt, host):
    depth = p["w_in"].shape[0]
    D = x.shape[2]
    dm = HEAD_DIM * MLSTM_HEADS
    ng = 4 * MLSTM_HEADS
    dh = p["hy_skip"].shape[1]
    o_g, o_u, o_ga = 4 * dm, 4 * dm + ng, 4 * dm + ng + 3 * dh
    row = lambda a: a.reshape(1, -1)
    col = lambda a: a.reshape(-1, 1)
    w_in = p["w_in"][l]
    b_in = p["b_in"][l]
    w_qvo = jnp.concatenate([w_in[:, :dm], w_in[:, 2 * dm:4 * dm]], axis=1)
    b_qvo = jnp.concatenate([b_in[:dm], b_in[2 * dm:4 * dm]])
    own = None if filt is not None else (x.shape[1], _filter_operands(p, l, x.shape[1]))
    outs = _inproj(
        x, row(p["norm_mix"][l]), w_qvo.astype(BF16), row(b_qvo),
        w_in[:, dm:2 * dm].T.astype(BF16), col(b_in[dm:2 * dm]),
        w_in[:, o_g:o_u].T.astype(BF16), col(b_in[o_g:o_u]),
        w_in[:, o_u:o_ga].astype(BF16), row(b_in[o_u:o_ga]),
        p["hy_conv_w"][l], row(p["hy_conv_b"][l]), own)
    q, kt, v, o, g_col, g_row, x0, zc = outs[:8]
    if own is not None:
        filt = outs[8:]
    h_fwd, h_bwd = _mlstm(q, kt, v, g_col, g_row)
    y = _long_conv(zc, *filt)
    x = _merge(x, h_fwd, h_bwd, o, y, zc, x0, row(p["mlstm_norm"][l]), row(p["hy_skip"][l]),
               row(p["norm_mix"][l]),
               w_in[:, o_ga:o_ga + D].astype(BF16), row(b_in[o_ga:o_ga + D]),
               w_in[:, o_ga + D:].astype(BF16), row(b_in[o_ga + D:]),
               p["w_branch_a"][l].astype(BF16), p["w_branch_b"][l].astype(BF16),
               p["w_out"][l].astype(BF16))
    out = _mlp(x, row(p["norm_mlp"][l]), p["w_mlp1"][l].astype(BF16), p["w_mlp2"][l].astype(BF16),
               row(p["norm_final"]), l == depth - 1, host)
    return (out[0], out[1:]) if host is not None else (out, None)


def _filter_operands(p, l, L):
    n2 = 2 * L // DFT_N1
    m1f = jnp.asarray(_dft_tables(DFT_N1, n2)["m1f"]).astype(BF16)
    return _filter_args(m1f, p["hy_w1"][l], p["hy_b1"][l], p["hy_w2"][l], p["hy_b2"][l], p["hy_w3"][l],
                        p["hy_b3"][l], p["hy_freq"][l], p["hy_w_fo"][l], p["hy_decay"][l])


def _run(xs, p):
    depth = p["w_in"].shape[0]
    order = sorted(range(len(xs)), key=lambda gi: xs[gi].shape[1])
    blocks = [(gi, l) for l in range(depth) for gi in order]
    xs = list(xs)
    filt = None
    for bi, (gi, l) in enumerate(blocks):
        host = None
        if bi + 1 < len(blocks):
            gn, ln = blocks[bi + 1]
            host = (xs[gn].shape[1], _filter_operands(p, ln, xs[gn].shape[1]))
        xs[gi], filt = _layer(xs[gi], p, l, filt, host)
    return tuple(xs)


def kernel(x_prompt, x_sample, norm_mix, w_in, b_in, mlstm_norm, hy_conv_w, hy_conv_b, hy_w1, hy_b1, hy_w2, hy_b2, hy_w3, hy_b3, hy_freq, hy_w_fo, hy_decay, hy_skip, w_branch_a, w_branch_b, w_out, norm_mlp, w_mlp1, w_mlp2, norm_final):
    p = dict(norm_mix=norm_mix, w_in=w_in, b_in=b_in, mlstm_norm=mlstm_norm, hy_conv_w=hy_conv_w,
             hy_conv_b=hy_conv_b, hy_w1=hy_w1, hy_b1=hy_b1, hy_w2=hy_w2, hy_b2=hy_b2, hy_w3=hy_w3,
             hy_b3=hy_b3, hy_freq=hy_freq, hy_w_fo=hy_w_fo, hy_decay=hy_decay, hy_skip=hy_skip,
             w_branch_a=w_branch_a, w_branch_b=w_branch_b, w_out=w_out, norm_mlp=norm_mlp,
             w_mlp1=w_mlp1, w_mlp2=w_mlp2, norm_final=norm_final)
    return _run((x_prompt, x_sample), p)
```

```python
import functools
import math

import numpy as np
import jax
import jax.numpy as jnp
from jax import lax
from jax.experimental import pallas as pl
from jax.experimental.pallas import tpu as pltpu

F32 = jnp.float32
BF16 = jnp.bfloat16

EPS = 1e-6
MLSTM_HEADS = 4
HEAD_DIM = 128
CHUNK = 256
HYENA_BANDS = 16
FILTER_HIDDEN = 64
DFT_N1 = 128
SUBLANES = 8
TOKEN_BLOCK = 512
VMEM_LIMIT = 48 * 1024 * 1024
HIGHEST = lax.Precision.HIGHEST


def _params(*sem):
    return pltpu.CompilerParams(dimension_semantics=sem, vmem_limit_bytes=VMEM_LIMIT)


def _rms(x, g):
    return x * lax.rsqrt(jnp.mean(x * x, axis=-1, keepdims=True) + EPS) * g


def _dot(a, b):
    return jnp.dot(a, b, preferred_element_type=F32)


def _dot_nt(a, b):
    return lax.dot_general(a, b, (((1,), (1,)), ((), ())), preferred_element_type=F32)


def _inproj_kernel(host_len, x_ref, xp_ref, xn_ref, g_ref, wm_ref, bm_ref, wkt_ref, bk_ref,
                   wgt_ref, bgt_ref, wu_ref, bu_ref, cw_ref, cb_ref, *rest):
    if host_len:
        (q_ref, kt_ref, v_ref, o_ref, gc_ref, gr_ref, x0_ref, zc_ref,
         af_ref, l1_ref, cb_s, sb_s) = rest[N_FILTER_REFS:]
        step = pl.program_id(0) * pl.num_programs(1) + pl.program_id(1)

        @pl.when(step == 0)
        def _():
            _filter_init(host_len, rest, l1_ref, cb_s, sb_s)

        gs = af_ref.shape[0]

        def store(s, val):
            af_ref[s] = val
        hosted = _filter_pass(host_len, rest[:N_FILTER_REFS], step * gs, gs, store, l1_ref, cb_s, sb_s)
    else:
        q_ref, kt_ref, v_ref, o_ref, gc_ref, gr_ref, x0_ref, zc_ref = rest
        hosted = iter(())
    i = pl.program_id(1)
    last = pl.num_programs(1) - 1
    g = g_ref[...]
    n = _rms(x_ref[0], g).astype(BF16)
    dm = HEAD_DIM * MLSTM_HEADS
    next(hosted, None)
    pm = _dot(n, wm_ref[...]) + bm_ref[...]
    q_ref[0] = (pm[:, :dm] * (HEAD_DIM ** -0.5)).astype(BF16)
    v_ref[0] = pm[:, dm:2 * dm].astype(BF16)
    o_ref[0] = pm[:, 2 * dm:]
    next(hosted, None)
    kt_ref[0] = (_dot_nt(wkt_ref[...], n) + bk_ref[...]).astype(BF16)
    gates = _dot_nt(wgt_ref[...], n) + bgt_ref[...]
    gr_ref[0] = gates
    gc_ref[0] = gates.T
    next(hosted, None)

    tm = n.shape[0]
    halo = _rms(jnp.concatenate([xp_ref[0], xn_ref[0]], axis=0), g).astype(BF16)
    u_all = _dot(jnp.concatenate([n, halo], axis=0), wu_ref[...]) + bu_ref[...]
    u = u_all[:tm]
    u_prev = jnp.where(i > 0, u_all[tm + SUBLANES - 1:tm + SUBLANES], 0.0)
    u_next = jnp.where(i < last, u_all[tm + SUBLANES:tm + SUBLANES + 1], 0.0)
    row = lax.broadcasted_iota(jnp.int32, (tm, 1), 0)
    u_m1 = jnp.where(row == 0, u_prev, pltpu.roll(u, 1, 0))
    u_p1 = jnp.where(row == tm - 1, u_next, pltpu.roll(u, tm - 1, 0))
    cw = cw_ref[...]
    c = u_m1 * cw[0:1] + u * cw[1:2] + u_p1 * cw[2:3] + cb_ref[...]
    dh = c.shape[1] // 3
    x0_ref[0] = c[:, :dh]
    zc_ref[0] = c[:, dh:2 * dh] * c[:, 2 * dh:]
    for _ in hosted:
        pass


def _inproj(x, g, wm, bm, wkt, bk, wgt, bgt, wu, bu, cw, cb, host=None):
    B, L, D = x.shape
    tm = min(TOKEN_BLOCK, L)
    nb = L // tm
    r8 = tm // SUBLANES
    dm = wkt.shape[0]
    dh = wu.shape[1] // 3
    ng = wgt.shape[0]
    ws = (g, wm, bm, wkt, bk, wgt, bgt, wu, bu, cw, cb)
    full = lambda a: pl.BlockSpec(a.shape, lambda b, i: (0,) * a.ndim)
    tok = lambda w: pl.BlockSpec((1, tm, w), lambda b, i: (b, i, 0))
    tr = lambda w: pl.BlockSpec((1, w, tm), lambda b, i: (b, 0, i))
    in_specs = [
        tok(D),
        pl.BlockSpec((1, SUBLANES, D), lambda b, i: (b, jnp.maximum(i * r8 - 1, 0), 0)),
        pl.BlockSpec((1, SUBLANES, D), lambda b, i: (b, jnp.minimum((i + 1) * r8, L // SUBLANES - 1), 0)),
    ] + [full(w) for w in ws]
    sds = jax.ShapeDtypeStruct
    out_shape = (sds((B, L, dm), BF16), sds((B, dm, L), BF16), sds((B, L, dm), BF16), sds((B, L, dm), F32),
                 sds((B, L, ng), F32), sds((B, ng, L), F32), sds((B, L, dh), F32), sds((B, L, dh), F32))
    out_specs = (tok(dm), tr(dm), tok(dm), tok(dm), tok(ng), tr(ng), tok(dh), tok(dh))
    sem, scratch, host_len, fargs = ("parallel", "parallel"), [], 0, ()
    if host is not None:
        host_len, fargs = host
        n2, C = fargs[0].shape[1], fargs[-1].shape[1]
        gs = DFT_N1 // (B * nb)
        assert gs * B * nb == DFT_N1
        in_specs += [full(a) for a in fargs]
        fshape, scratch = _filter_out(n2, C)
        out_specs += (pl.BlockSpec((gs, n2, C), lambda b, i: (b * nb + i, 0, 0)),
                      pl.BlockSpec((1, C), lambda b, i: (0, 0)))
        out_shape += fshape
        sem = ("arbitrary", "arbitrary")
    return pl.pallas_call(
        functools.partial(_inproj_kernel, host_len), grid=(B, nb), in_specs=in_specs,
        out_specs=out_specs, out_shape=out_shape, scratch_shapes=scratch,
        compiler_params=_params(*sem), name="inproj")(x, x, x, *ws, *fargs)


def _log_sigmoid(x):
    return jnp.minimum(x, 0.0) - jnp.log1p(jnp.exp(-jnp.abs(x)))


def _mlstm_dir(reverse, q_ref, kt_ref, v_ref, gc_ref, gr_ref, out_ref, c_s, m_s, slot):
    nh = MLSTM_HEADS
    nb = q_ref.shape[0]
    nq = nb * nh
    ch = q_ref.shape[1]
    off = 2 * nh if reverse else 0
    ri = lax.broadcasted_iota(jnp.int32, (ch, ch), 0)
    ci = lax.broadcasted_iota(jnp.int32, (ch, ch), 1)
    mask = (ri <= ci) if reverse else (ci <= ri)
    tri_row = ((ci <= ri) if reverse else (ri <= ci)).astype(F32)

    li_col = jnp.concatenate([gc_ref[b][:, off:off + nh] for b in range(nb)], axis=1)
    li_row = jnp.concatenate([gr_ref[b][off:off + nh, :] for b in range(nb)], axis=0)
    lf_row = _log_sigmoid(jnp.concatenate([gr_ref[b][off + nh:off + 2 * nh, :] for b in range(nb)], axis=0))
    b_row = jnp.dot(lf_row, tri_row, precision=HIGHEST, preferred_element_type=F32)
    b_col = b_row.T
    beta_col = li_col - b_col
    beta_row = li_row - b_row
    cmx = beta_col
    rowi = lax.broadcasted_iota(jnp.int32, (ch, nq), 0)
    k = 1
    while k < ch:
        if reverse:
            cmx = jnp.maximum(cmx, jnp.where(rowi < ch - k, pltpu.roll(cmx, ch - k, 0), -jnp.inf))
        else:
            cmx = jnp.maximum(cmx, jnp.where(rowi >= k, pltpu.roll(cmx, k, 0), -jnp.inf))
        k *= 2
    last = 0 if reverse else ch - 1
    g_row = b_col[last:last + 1, :]
    m_loc = g_row + cmx[last:last + 1, :]
    m_prev = m_s[slot:slot + 1, 0:nq]
    mx = jnp.maximum(m_prev, cmx)
    s_inter = jnp.exp(m_prev - mx)
    inv_floor = jnp.exp(-mx - b_col)
    m_new = jnp.maximum(g_row + m_prev, m_loc)
    s_old = jnp.exp(g_row + m_prev - m_new)
    s_new = jnp.exp(m_loc - m_new)
    w_row = jnp.exp(beta_row - jnp.max(beta_row, axis=1, keepdims=True))
    m_s[slot:slot + 1, 0:nq] = m_new

    ones_col = (lax.broadcasted_iota(jnp.int32, (ch, HEAD_DIM), 1) == 0).astype(BF16)
    for b in range(nb):
        for h in range(nh):
            i = b * nh + h
            hs = slice(h * HEAD_DIM, (h + 1) * HEAD_DIM)
            q = q_ref[b, :, hs]
            kt = kt_ref[b, hs, :]
            v_aug = jnp.concatenate([v_ref[b, :, hs], ones_col], axis=1)
            d_mat = jnp.exp(jnp.where(mask, beta_row[i:i + 1, :] - mx[:, i:i + 1], -jnp.inf))
            scores = (_dot(q, kt) * d_mat).astype(BF16)
            c_prev = c_s[slot * nq + i]
            num = _dot(scores, v_aug) + s_inter[:, i:i + 1] * _dot(q, c_prev.astype(BF16))
            den = num[:, HEAD_DIM:HEAD_DIM + 1]
            out_ref[b, :, hs] = num[:, :HEAD_DIM] * (1.0 / jnp.maximum(jnp.abs(den), inv_floor[:, i:i + 1]))
            ktw = (kt.astype(F32) * w_row[i:i + 1, :]).astype(BF16)
            c_s[slot * nq + i] = s_old[:, i:i + 1] * c_prev + s_new[:, i:i + 1] * _dot(ktw, v_aug)


def _mlstm_kernel(qf, ktf, vf, gcf, grf, qb, ktb, vb, gcb, grb, of, ob, c_s, m_s):
    @pl.when(pl.program_id(0) == 0)
    def _():
        c_s[...] = jnp.zeros_like(c_s)
        m_s[...] = jnp.zeros_like(m_s)

    _mlstm_dir(False, qf, ktf, vf, gcf, grf, of, c_s, m_s, 0)
    _mlstm_dir(True, qb, ktb, vb, gcb, grb, ob, c_s, m_s, 1)


def _mlstm(q, kt, v, g_col, g_row):
    B, L, dm = q.shape
    ch = min(CHUNK, L)
    nc = L // ch
    specs = []
    for cidx in ((lambda c: c), (lambda c: nc - 1 - c)):
        tok = lambda w, cidx=cidx: pl.BlockSpec((B, ch, w), lambda c: (0, cidx(c), 0))
        tr = lambda w, cidx=cidx: pl.BlockSpec((B, w, ch), lambda c: (0, 0, cidx(c)))
        specs.append(([tok(dm), tr(dm), tok(dm), tok(g_col.shape[2]), tr(g_row.shape[1])], tok(dm)))
    args = [q, kt, v, g_col, g_row]
    out = jax.ShapeDtypeStruct((B, L, dm), F32)
    return pl.pallas_call(
        _mlstm_kernel, grid=(nc,), in_specs=specs[0][0] + specs[1][0],
        out_specs=(specs[0][1], specs[1][1]), out_shape=(out, out),
        scratch_shapes=[pltpu.VMEM((2 * B * MLSTM_HEADS, HEAD_DIM, 2 * HEAD_DIM), F32),
                        pltpu.VMEM((SUBLANES, 128), F32)],
        compiler_params=_params("arbitrary"), name="mlstm")(*args, *args)


N_FILTER_REFS = 12


def _filter_init(seq_len, refs, l1_ref, cb_s, sb_s):
    half = refs[0].shape[1] // 2
    hid = FILTER_HIDDEN
    grp = lax.broadcasted_iota(jnp.int32, (half, 2 * hid), 1) >= hid
    r2 = lax.broadcasted_iota(jnp.int32, (half, 2 * hid), 0)
    l1_ref[...] = jnp.zeros_like(l1_ref)
    lag0 = DFT_N1 * jnp.where(grp, half - r2, r2)
    ang = refs[1][...] * (lag0.astype(F32) * ((2.0 * math.pi) / seq_len))
    cb_s[...] = jnp.cos(ang)
    sb_s[...] = jnp.sin(ang)


def _filter_pass(seq_len, refs, n1_first, gs, store, l1_ref, cb_s, sb_s):
    (m_ref, fvec_ref, ph_ref, w1_ref, b1_ref, w2_ref, b2_ref, w3_ref, b3_ref, fr_ref, wf_ref,
     dec_ref) = refs
    half = m_ref.shape[1] // 2
    hid = FILTER_HIDDEN
    C = dec_ref.shape[1]
    L = seq_len
    lane = lax.broadcasted_iota(jnp.int32, (half, 2 * hid), 1)
    grp = lane >= hid
    feat = lane & (hid - 1)
    r2 = lax.broadcasted_iota(jnp.int32, (half, 2 * hid), 0)
    r_col = lax.broadcasted_iota(jnp.int32, (half, 1), 0)
    fvec = fvec_ref[...]
    fr = fr_ref[...]
    sign = jnp.where(grp[0:1], -1.0, 1.0)
    scale = (2.0 * math.pi) / L

    def lag(n):
        return jnp.where(n < L, n, jnp.where(n == L, 0, 2 * L - n)).astype(F32)

    grp_s = jnp.concatenate([grp] * gs, axis=0)
    zs = []
    for s in range(gs):
        n1 = n1_first + s
        n = n1 + DFT_N1 * (r2 + jnp.where(grp, half, 0))
        ang1 = sign * (fvec * (n1.astype(F32) * scale)) + ph_ref[...]
        feats = jnp.cos(ang1) * cb_s[...] - jnp.sin(ang1) * sb_s[...]
        feats = jnp.where(n == L, jnp.cos(ph_ref[...]), feats)
        zs.append(jnp.where(feat == 0, lag(n) / (L - 1),
                            jnp.where(feat <= 2 * HYENA_BANDS, feats, 0.0)))
    z = jnp.concatenate(zs, axis=0)
    h = jnp.sin(fr * (_dot(z.astype(BF16), w1_ref[...]) + b1_ref[...]))
    yield
    h = jnp.sin(fr * (_dot(h.astype(BF16), w2_ref[...]) + b2_ref[...]))
    yield
    h = jnp.sin(fr * (_dot(h.astype(BF16), w3_ref[...]) + b3_ref[...]))
    yield
    taps = []
    for gi in range(2):
        cs = slice(gi * C, (gi + 1) * C)
        hg = jnp.where(grp_s == (gi == 1), h, 0.0)
        tp = _dot(hg.astype(BF16), wf_ref[:, cs])
        dec = jnp.abs(dec_ref[gi:gi + 1, :])
        parts = []
        for s in range(gs):
            n_col = n1_first + s + DFT_N1 * (r_col + gi * half)
            tps = tp[s * half:(s + 1) * half] * jnp.exp(-(lag(n_col) / (L - 1)) * dec)
            l1_ref[...] += jnp.sum(jnp.abs(tps), axis=0, keepdims=True)
            parts.append(jnp.where(n_col == L, 0.0, tps).astype(BF16))
        taps.append(parts)
    for s in range(gs):
        store(s, _pack(_dot(m_ref[...], jnp.concatenate([taps[0][s], taps[1][s]], axis=0))))


def _blockdiag2(w):
    z = jnp.zeros_like(w)
    return jnp.concatenate([jnp.concatenate([w, z], axis=1), jnp.concatenate([z, w], axis=1)], axis=0)


def _filter_args(m1, w1, b1, w2, b2, w3, b3, fr, wfo, dec):
    hid = FILTER_HIDDEN
    fgrid = jnp.linspace(1e-4, HYENA_BANDS - 1, HYENA_BANDS, dtype=F32)
    fhalf = jnp.zeros((hid,), F32).at[1:1 + HYENA_BANDS].set(fgrid).at[1 + HYENA_BANDS:1 + 2 * HYENA_BANDS].set(fgrid)
    phalf = jnp.zeros((hid,), F32).at[1 + HYENA_BANDS:1 + 2 * HYENA_BANDS].set(0.5 * math.pi)
    two = lambda a: jnp.concatenate([a.reshape(1, -1), a.reshape(1, -1)], axis=1)
    w1p = jnp.zeros((hid, hid), F32).at[:w1.shape[0]].set(w1)
    mats = [_blockdiag2(wmat).astype(BF16) for wmat in (w1p, w2, w3)]
    wst = jnp.concatenate([wfo, wfo], axis=0).astype(BF16)
    args = (m1, two(fhalf), two(phalf), mats[0], two(b1), mats[1], two(b2), mats[2], two(b3), two(fr),
            wst, dec)
    assert len(args) == N_FILTER_REFS
    return args


def _filter_out(n2, C):
    shapes = (jax.ShapeDtypeStruct((DFT_N1, n2, C), jnp.int32), jax.ShapeDtypeStruct((1, C), F32))
    scratch = [pltpu.VMEM((n2 // 2, 2 * FILTER_HIDDEN), F32), pltpu.VMEM((n2 // 2, 2 * FILTER_HIDDEN), F32)]
    return shapes, scratch


@functools.lru_cache(maxsize=None)
def _dft_tables(n1, n2):
    n = n1 * n2
    h = n2 // 2
    a2 = 2.0 * np.pi * np.outer(np.arange(n2), np.arange(n2)) / n2
    c2, s2 = np.cos(a2), np.sin(a2)
    m1d = np.empty((n2, 2, n2))
    m1d[:, 0, :h], m1d[:, 0, h:] = c2[:, :h], s2[:, :h]
    m1d[:, 1, :h], m1d[:, 1, h:] = -s2[:, :h], c2[:, :h]
    m1f = np.stack([c2, -s2], axis=1)
    a1 = 2.0 * np.pi * np.outer(np.arange(n1), np.arange(n1)) / n1
    f1r, f1i = np.cos(a1), -np.sin(a1)
    il = lambda m: m.reshape(m.shape[0], 2, -1).swapaxes(1, 2).reshape(m.shape[0], -1)
    pg = il(np.block([[f1r, -f1i], [f1i, f1r]]))
    pgs = il(np.block([[-f1i, -f1r], [f1r, -f1i]]))
    qd = il(np.block([[f1r, f1i], [-f1i, f1r]]).T).T
    at = 2.0 * np.pi * np.outer(np.arange(n2), np.arange(n1)) / n
    tr, ti = np.cos(at), -np.sin(at)
    txr = np.repeat(tr, 2, axis=1)
    txi = np.repeat(ti, 2, axis=1)
    er, ei = c2[:h] / n, s2[:h] / n
    pp = il(np.block([[er, -ei], [ei, er]]))
    pps = il(np.block([[-ei, -er], [er, -ei]]))
    ur, ui = np.cos(at).T, np.sin(at).T
    uxr = np.repeat(ur, 2, axis=1)
    uxi = np.repeat(ui, 2, axis=1)
    f32 = lambda a: np.asarray(a, np.float32)
    return dict(m1d=f32(m1d.reshape(2 * n2, n2)), m1f=f32(m1f.reshape(2 * n2, n2)),
                pg=f32(pg), pgs=f32(pgs), qd=f32(qd), txr=f32(txr), txi=f32(txi),
                pp=f32(pp), pps=f32(pps), uxr=f32(uxr), uxi=f32(uxi))


def _pack(x):
    return pltpu.bitcast(x.astype(BF16), jnp.int32)


def _unpack(w):
    return pltpu.bitcast(w, BF16)


def _dft1_kernel(m_ref, x_ref, out_ref):
    nt = out_ref.shape[0]
    m = m_ref[...]
    for j in range(nt):
        parts = [x_ref[b, :, j, :] for b in range(x_ref.shape[0])]
        x = parts[0] if len(parts) == 1 else jnp.concatenate(parts, axis=0)
        out_ref[j] = _pack(_dot(m, x.astype(BF16)))


def _dft1(m1, x):
    parts, rows, n1, C = x.shape
    nt = SUBLANES
    return pl.pallas_call(
        _dft1_kernel, grid=(n1 // nt,),
        in_specs=[pl.BlockSpec(m1.shape, lambda i: (0, 0)),
                  pl.BlockSpec((parts, rows, nt, C), lambda i: (0, 0, i, 0))],
        out_specs=pl.BlockSpec((nt, m1.shape[0] // 2, C), lambda i: (i, 0, 0)),
        out_shape=jax.ShapeDtypeStruct((n1, m1.shape[0] // 2, C), jnp.int32),
        compiler_params=_params("parallel"), name="dft_stage1")(m1, x)


def _conv_mid_kernel(a_ref, af_ref, l1_ref, pg_ref, pgs_ref, txr_ref, txi_ref, qd_ref, out_ref):
    n1 = a_ref.shape[0]
    qd = qd_ref[...]
    inv = 1.0 / l1_ref[...]
    for j in range(out_ref.shape[0]):
        gd = (pg_ref[...] * txr_ref[j:j + 1, :] + pgs_ref[...] * txi_ref[j:j + 1, :]).astype(BF16)
        x = _dot(gd, _unpack(a_ref[:, j, :]))
        kf = _dot(gd, _unpack(af_ref[:, j, :])) * inv
        xr, xi = x[:n1], x[n1:]
        kr, ki = kf[:n1], kf[n1:]
        y = jnp.concatenate([xr * kr - xi * ki, xr * ki + xi * kr], axis=0).astype(BF16)
        out_ref[j] = _pack(_dot(qd, y))


def _conv_mid(a, af, l1, tabs):
    n1, n2, C = a.shape
    kt = SUBLANES
    ablk = pl.BlockSpec((n1, kt, C), lambda i: (0, i, 0))
    mat = pl.BlockSpec((2 * n1, 2 * n1), lambda i: (0, 0))
    tw = pl.BlockSpec((kt, 2 * n1), lambda i: (i, 0))
    return pl.pallas_call(
        _conv_mid_kernel, grid=(n2 // kt,),
        in_specs=[ablk, ablk, pl.BlockSpec((1, C), lambda i: (0, 0)), mat, mat, tw, tw, mat],
        out_specs=pl.BlockSpec((kt, n1, C), lambda i: (i, 0, 0)),
        out_shape=jax.ShapeDtypeStruct((n2, n1, C), jnp.int32),
        compiler_params=_params("parallel"), name="conv_mid")(
            a, af, l1, tabs["pg"], tabs["pgs"], tabs["txr"], tabs["txi"], tabs["qd"])


def _conv_out_kernel(b_ref, pp_ref, pps_ref, uxr_ref, uxi_ref, out_ref):
    h = out_ref.shape[1]
    for j in range(out_ref.shape[2]):
        hm = (pp_ref[...] * uxr_ref[j:j + 1, :] + pps_ref[...] * uxi_ref[j:j + 1, :]).astype(BF16)
        y = _dot(hm, _unpack(b_ref[:, j, :]))
        out_ref[0, :, j, :] = y[:h]
        out_ref[1, :, j, :] = y[h:]


def _conv_out(bm, tabs):
    n2, n1, C = bm.shape
    nt = SUBLANES
    mat = pl.BlockSpec((n2, 2 * n2), lambda i: (0, 0))
    tw = pl.BlockSpec((nt, 2 * n2), lambda i: (i, 0))
    return pl.pallas_call(
        _conv_out_kernel, grid=(n1 // nt,),
        in_specs=[pl.BlockSpec((n2, nt, C), lambda i: (0, i, 0)), mat, mat, tw, tw],
        out_specs=pl.BlockSpec((2, n2 // 2, nt, C), lambda i: (0, 0, i, 0)),
        out_shape=jax.ShapeDtypeStruct((2, n2 // 2, n1, C), F32),
        compiler_params=_params("parallel"), name="conv_out")(
            bm, tabs["pp"], tabs["pps"], tabs["uxr"], tabs["uxi"])


def _long_conv(zc, af, l1):
    B, L, C = zc.shape
    assert B == 2, "the two batch rows are packed as one complex sequence"
    n1 = DFT_N1
    n2 = 2 * L // n1
    tabs = {k: jnp.asarray(v) for k, v in _dft_tables(n1, n2).items()}
    a = _dft1(tabs["m1d"].astype(BF16), zc.reshape(B, n2 // 2, n1, C))
    bm = _conv_mid(a, af, l1, tabs)
    return _conv_out(bm, tabs).reshape(B, L, C)


def _merge_kernel(x_ref, hf_ref, hb_ref, o_ref, y_ref, zc_ref, x0_ref, hn_ref, skip_ref, g_ref,
                  wga_ref, bga_ref, wgb_ref, bgb_ref, wa_ref, wb_ref, wo_ref, out_ref):
    x = x_ref[0]
    n = _rms(x, g_ref[...]).astype(BF16)
    ga = jax.nn.sigmoid(_dot(n, wga_ref[...]) + bga_ref[...])
    gb = jax.nn.sigmoid(_dot(n, wgb_ref[...]) + bgb_ref[...])
    heads = []
    for h in range(MLSTM_HEADS):
        hs = slice(h * HEAD_DIM, (h + 1) * HEAD_DIM)
        s = hf_ref[0, :, hs] + hb_ref[0, :, hs]
        dv = s - jnp.mean(s, axis=1, keepdims=True)
        var = jnp.mean(dv * dv, axis=1, keepdims=True)
        heads.append((dv * lax.rsqrt(var + EPS) * hn_ref[:, hs]
                      * jax.nn.sigmoid(o_ref[0, :, hs])).astype(BF16))
    ml = jnp.concatenate(heads, axis=1)
    hy = (x0_ref[0] * (y_ref[0] + skip_ref[...] * zc_ref[0])).astype(BF16)
    merged = ga * _dot(ml, wa_ref[...]) + gb * _dot(hy, wb_ref[...])
    out_ref[0] = x + _dot(merged.astype(BF16), wo_ref[...])


def _merge(x, hf, hb, o, y, zc, x0, hn, skip, g, wga, bga, wgb, bgb, wa, wb, wo):
    B, L, D = x.shape
    tm = min(TOKEN_BLOCK, L)
    full = lambda w: pl.BlockSpec(w.shape, lambda bb, i: (0,) * w.ndim)
    tok = lambda w: pl.BlockSpec((1, tm, w), lambda bb, i: (bb, i, 0))
    ws = (hn, skip, g, wga, bga, wgb, bgb, wa, wb, wo)
    acts = (hf, hb, o, y, zc, x0)
    return pl.pallas_call(
        _merge_kernel, grid=(B, L // tm),
        in_specs=[tok(D)] + [tok(a.shape[2]) for a in acts] + [full(w) for w in ws],
        out_specs=tok(D), out_shape=jax.ShapeDtypeStruct(x.shape, F32),
        compiler_params=_params("parallel", "parallel"), name="merge")(x, *acts, *ws)


def _mlp_kernel(final, nchunk, host_len, x_ref, g_ref, w1_ref, w2_ref, gf_ref, *rest):
    out_ref = rest[N_FILTER_REFS] if host_len else rest[0]
    if host_len:
        af_ref, l1_ref, cb_s, sb_s = rest[N_FILTER_REFS + 1:]
        step = pl.program_id(0) * pl.num_programs(1) + pl.program_id(1)

        @pl.when(step == 0)
        def _():
            _filter_init(host_len, rest, l1_ref, cb_s, sb_s)

        gs = af_ref.shape[0]

        def store(s, val):
            af_ref[s] = val
        hosted = _filter_pass(host_len, rest[:N_FILTER_REFS], step * gs, gs, store, l1_ref, cb_s, sb_s)
    else:
        hosted = iter(())

    x = x_ref[0]
    n = _rms(x, g_ref[...]).astype(BF16)
    fc = w1_ref.shape[1] // nchunk
    acc = x
    for c in range(nchunk):
        next(hosted, None)
        hcl = jnp.maximum(_dot(n, w1_ref[:, c * fc:(c + 1) * fc]), 0.0)
        acc = acc + _dot((hcl * hcl).astype(BF16), w2_ref[c * fc:(c + 1) * fc, :])
    for _ in hosted:
        pass
    out_ref[0] = _rms(acc, gf_ref[...]) if final else acc


def _mlp(x, g, w1, w2, gf, final, host=None):
    B, L, D = x.shape
    tm = min(TOKEN_BLOCK, L)
    steps = B * (L // tm)
    full = lambda w: pl.BlockSpec(w.shape, lambda bb, i: (0,) * w.ndim)
    once = lambda w: pl.BlockSpec(w.shape, lambda bb, i: (0,) * w.ndim, pipeline_mode=pl.Buffered(1))
    tok = pl.BlockSpec((1, tm, D), lambda bb, i: (bb, i, 0))
    args = [x, g, w1, w2, gf]
    in_specs = [tok, full(g), once(w1), once(w2), full(gf)]
    out_specs, out_shape, scratch = tok, jax.ShapeDtypeStruct(x.shape, F32), []
    sem = ("parallel", "parallel")
    host_len = 0
    if host is not None:
        host_len, fargs = host
        n2 = fargs[0].shape[1]
        C = fargs[-1].shape[1]
        gs = DFT_N1 // steps
        assert gs * steps == DFT_N1
        args += list(fargs)
        in_specs += [full(a) for a in fargs]
        per_l = L // tm
        fshape, scratch = _filter_out(n2, C)
        out_specs = (tok, pl.BlockSpec((gs, n2, C), lambda bb, i: (bb * per_l + i, 0, 0)),
                     pl.BlockSpec((1, C), lambda bb, i: (0, 0)))
        out_shape = (out_shape,) + fshape
        sem = ("arbitrary", "arbitrary")
    return pl.pallas_call(
        functools.partial(_mlp_kernel, final, 4, host_len), grid=(B, L // tm),
        in_specs=in_specs, out_specs=out_specs, out_shape=out_shape, scratch_shapes=scratch,
        compiler_params=_params(*sem), name="mlp")(*args)


def _layer(x, p, l, filt, host):
    depth = p["w_in"].shape[0]
    D = x.shape[2]
    dm = HEAD_DIM * MLSTM_HEADS
    ng = 4 * MLSTM_HEADS
    dh = p["hy_skip"].shape[1]
    o_g, o_u, o_ga = 4 * dm, 4 * dm + ng, 4 * dm + ng + 3 * dh
    row = lambda a: a.reshape(1, -1)
    col = lambda a: a.reshape(-1, 1)
    w_in = p["w_in"][l]
    b_in = p["b_in"][l]
    w_qvo = jnp.concatenate([w_in[:, :dm], w_in[:, 2 * dm:4 * dm]], axis=1)
    b_qvo = jnp.concatenate([b_in[:dm], b_in[2 * dm:4 * dm]])
    own = None if filt is not None else (x.shape[1], _filter_operands(p, l, x.shape[1]))
    outs = _inproj(
        x, row(p["norm_mix"][l]), w_qvo.astype(BF16), row(b_qvo),
        w_in[:, dm:2 * dm].T.astype(BF16), col(b_in[dm:2 * dm]),
        w_in[:, o_g:o_u].T.astype(BF16), col(b_in[o_g:o_u]),
        w_in[:, o_u:o_ga].astype(BF16), row(b_in[o_u:o_ga]),
        p["hy_conv_w"][l], row(p["hy_conv_b"][l]), own)
    q, kt, v, o, g_col, g_row, x0, zc = outs[:8]
    if own is not None:
        filt = outs[8:]
    h_fwd, h_bwd = _mlstm(q, kt, v, g_col, g_row)
    y = _long_conv(zc, *filt)
    x = _merge(x, h_fwd, h_bwd, o, y, zc, x0, row(p["mlstm_norm"][l]), row(p["hy_skip"][l]),
               row(p["norm_mix"][l]),
               w_in[:, o_ga:o_ga + D].astype(BF16), row(b_in[o_ga:o_ga + D]),
               w_in[:, o_ga + D:].astype(BF16), row(b_in[o_ga + D:]),
               p["w_branch_a"][l].astype(BF16), p["w_branch_b"][l].astype(BF16),
               p["w_out"][l].astype(BF16))
    out = _mlp(x, row(p["norm_mlp"][l]), p["w_mlp1"][l].astype(BF16), p["w_mlp2"][l].astype(BF16),
               row(p["norm_final"]), l == depth - 1, host)
    return (out[0], out[1:]) if host is not None else (out, None)


def _filter_operands(p, l, L):
    n2 = 2 * L // DFT_N1
    m1f = jnp.asarray(_dft_tables(DFT_N1, n2)["m1f"]).astype(BF16)
    return _filter_args(m1f, p["hy_w1"][l], p["hy_b1"][l], p["hy_w2"][l], p["hy_b2"][l], p["hy_w3"][l],
                        p["hy_b3"][l], p["hy_freq"][l], p["hy_w_fo"][l], p["hy_decay"][l])


def _run(xs, p):
    depth = p["w_in"].shape[0]
    order = sorted(range(len(xs)), key=lambda gi: xs[gi].shape[1])
    blocks = [(gi, l) for l in range(depth) for gi in order]
    xs = list(xs)
    filt = None
    for bi, (gi, l) in enumerate(blocks):
        host = None
        if bi + 1 < len(blocks):
            gn, ln = blocks[bi + 1]
            host = (xs[gn].shape[1], _filter_operands(p, ln, xs[gn].shape[1]))
        xs[gi], filt = _layer(xs[gi], p, l, filt, host)
    return tuple(xs)


def kernel(x_prompt, x_sample, norm_mix, w_in, b_in, mlstm_norm, hy_conv_w, hy_conv_b, hy_w1, hy_b1, hy_w2, hy_b2, hy_w3, hy_b3, hy_freq, hy_w_fo, hy_decay, hy_skip, w_branch_a, w_branch_b, w_out, norm_mlp, w_mlp1, w_mlp2, norm_final):
    p = dict(norm_mix=norm_mix, w_in=w_in, b_in=b_in, mlstm_norm=mlstm_norm, hy_conv_w=hy_conv_w,
             hy_conv_b=hy_conv_b, hy_w1=hy_w1, hy_b1=hy_b1, hy_w2=hy_w2, hy_b2=hy_b2, hy_w3=hy_w3,
             hy_b3=hy_b3, hy_freq=hy_freq, hy_w_fo=hy_w_fo, hy_decay=hy_decay, hy_skip=hy_skip,
             w_branch_a=w_branch_a, w_branch_b=w_branch_b, w_out=w_out, norm_mlp=norm_mlp,
             w_mlp1=w_mlp1, w_mlp2=w_mlp2, norm_final=norm_final)
    return _run((x_prompt, x_sample), p)
```

```python
import functools
import math

import numpy as np
import jax
import jax.numpy as jnp
from jax import lax
from jax.experimental import pallas as pl
from jax.experimental.pallas import tpu as pltpu

F32 = jnp.float32
BF16 = jnp.bfloat16

EPS = 1e-6
MLSTM_HEADS = 4
HEAD_DIM = 128
CHUNK = 256
HYENA_BANDS = 16
FILTER_HIDDEN = 64
DFT_N1 = 128
SUBLANES = 8
TOKEN_BLOCK = 512
INPROJ_BLOCK = 1024
VMEM_LIMIT = 48 * 1024 * 1024
HIGHEST = lax.Precision.HIGHEST


def _params(*sem):
    return pltpu.CompilerParams(dimension_semantics=sem, vmem_limit_bytes=VMEM_LIMIT)


def _rms(x, g):
    return x * lax.rsqrt(jnp.mean(x * x, axis=-1, keepdims=True) + EPS) * g


def _dot(a, b):
    return jnp.dot(a, b, preferred_element_type=F32)


def _dot_nt(a, b):
    return lax.dot_general(a, b, (((1,), (1,)), ((), ())), preferred_element_type=F32)


def _inproj_kernel(host_len, x_ref, xp_ref, xn_ref, g_ref, wm_ref, bm_ref, wkt_ref, bk_ref,
                   wgt_ref, bgt_ref, wu_ref, bu_ref, cw_ref, cb_ref, *rest):
    if host_len:
        (q_ref, kt_ref, v_ref, o_ref, gc_ref, gr_ref, x0_ref, zc_ref,
         af_ref, l1_ref, cb_s, sb_s) = rest[N_FILTER_REFS:]
        step = pl.program_id(0) * pl.num_programs(1) + pl.program_id(1)

        @pl.when(step == 0)
        def _():
            _filter_init(host_len, rest, l1_ref, cb_s, sb_s)

        gs = af_ref.shape[0]

        def store(s, val):
            af_ref[s] = val
        hosted = _filter_pass(host_len, rest[:N_FILTER_REFS], step * gs, gs, store, l1_ref, cb_s, sb_s)
    else:
        q_ref, kt_ref, v_ref, o_ref, gc_ref, gr_ref, x0_ref, zc_ref = rest
        hosted = iter(())
    i = pl.program_id(1)
    last = pl.num_programs(1) - 1
    g = g_ref[...]
    n = _rms(x_ref[0], g).astype(BF16)
    dm = HEAD_DIM * MLSTM_HEADS
    next(hosted, None)
    pm = _dot(n, wm_ref[...]) + bm_ref[...]
    q_ref[0] = (pm[:, :dm] * (HEAD_DIM ** -0.5)).astype(BF16)
    v_ref[0] = pm[:, dm:2 * dm].astype(BF16)
    o_ref[0] = pm[:, 2 * dm:]
    next(hosted, None)
    kt_ref[0] = (_dot_nt(wkt_ref[...], n) + bk_ref[...]).astype(BF16)
    gates = _dot_nt(wgt_ref[...], n) + bgt_ref[...]
    gr_ref[0] = gates
    gc_ref[0] = gates.T
    next(hosted, None)

    tm = n.shape[0]
    halo = _rms(jnp.concatenate([xp_ref[0], xn_ref[0]], axis=0), g).astype(BF16)
    u_all = _dot(jnp.concatenate([n, halo], axis=0), wu_ref[...]) + bu_ref[...]
    u = u_all[:tm]
    u_prev = jnp.where(i > 0, u_all[tm + SUBLANES - 1:tm + SUBLANES], 0.0)
    u_next = jnp.where(i < last, u_all[tm + SUBLANES:tm + SUBLANES + 1], 0.0)
    row = lax.broadcasted_iota(jnp.int32, (tm, 1), 0)
    u_m1 = jnp.where(row == 0, u_prev, pltpu.roll(u, 1, 0))
    u_p1 = jnp.where(row == tm - 1, u_next, pltpu.roll(u, tm - 1, 0))
    cw = cw_ref[...]
    c = u_m1 * cw[0:1] + u * cw[1:2] + u_p1 * cw[2:3] + cb_ref[...]
    dh = c.shape[1] // 3
    x0_ref[0] = c[:, :dh]
    zc_ref[0] = c[:, dh:2 * dh] * c[:, 2 * dh:]
    for _ in hosted:
        pass


def _inproj(x, g, wm, bm, wkt, bk, wgt, bgt, wu, bu, cw, cb, host=None):
    B, L, D = x.shape
    tm = min(INPROJ_BLOCK, L)
    nb = L // tm
    r8 = tm // SUBLANES
    dm = wkt.shape[0]
    dh = wu.shape[1] // 3
    ng = wgt.shape[0]
    ws = (g, wm, bm, wkt, bk, wgt, bgt, wu, bu, cw, cb)
    full = lambda a: pl.BlockSpec(a.shape, lambda b, i: (0,) * a.ndim, pipeline_mode=pl.Buffered(1))
    tok = lambda w: pl.BlockSpec((1, tm, w), lambda b, i: (b, i, 0))
    tr = lambda w: pl.BlockSpec((1, w, tm), lambda b, i: (b, 0, i))
    in_specs = [
        tok(D),
        pl.BlockSpec((1, SUBLANES, D), lambda b, i: (b, jnp.maximum(i * r8 - 1, 0), 0)),
        pl.BlockSpec((1, SUBLANES, D), lambda b, i: (b, jnp.minimum((i + 1) * r8, L // SUBLANES - 1), 0)),
    ] + [full(w) for w in ws]
    sds = jax.ShapeDtypeStruct
    out_shape = (sds((B, L, dm), BF16), sds((B, dm, L), BF16), sds((B, L, dm), BF16), sds((B, L, dm), F32),
                 sds((B, L, ng), F32), sds((B, ng, L), F32), sds((B, L, dh), F32), sds((B, L, dh), F32))
    out_specs = (tok(dm), tr(dm), tok(dm), tok(dm), tok(ng), tr(ng), tok(dh), tok(dh))
    sem, scratch, host_len, fargs = ("parallel", "parallel"), [], 0, ()
    if host is not None:
        host_len, fargs = host
        n2, C = fargs[0].shape[1], fargs[-1].shape[1]
        gs = DFT_N1 // (B * nb)
        assert gs * B * nb == DFT_N1
        in_specs += [full(a) for a in fargs]
        fshape, scratch = _filter_out(n2, C)
        out_specs += (pl.BlockSpec((gs, n2, C), lambda b, i: (b * nb + i, 0, 0)),
                      pl.BlockSpec((1, C), lambda b, i: (0, 0)))
        out_shape += fshape
        sem = ("arbitrary", "arbitrary")
    return pl.pallas_call(
        functools.partial(_inproj_kernel, host_len), grid=(B, nb), in_specs=in_specs,
        out_specs=out_specs, out_shape=out_shape, scratch_shapes=scratch,
        compiler_params=_params(*sem), name="inproj")(x, x, x, *ws, *fargs)


def _log_sigmoid(x):
    return jnp.minimum(x, 0.0) - jnp.log1p(jnp.exp(-jnp.abs(x)))


def _mlstm_dir(reverse, q_ref, kt_ref, v_ref, gc_ref, gr_ref, out_ref, c_s, m_s, slot):
    nh = MLSTM_HEADS
    nb = q_ref.shape[0]
    nq = nb * nh
    ch = q_ref.shape[1]
    off = 2 * nh if reverse else 0
    ri = lax.broadcasted_iota(jnp.int32, (ch, ch), 0)
    ci = lax.broadcasted_iota(jnp.int32, (ch, ch), 1)
    mask = (ri <= ci) if reverse else (ci <= ri)
    tri_row = ((ci <= ri) if reverse else (ri <= ci)).astype(F32)

    li_col = jnp.concatenate([gc_ref[b][:, off:off + nh] for b in range(nb)], axis=1)
    li_row = jnp.concatenate([gr_ref[b][off:off + nh, :] for b in range(nb)], axis=0)
    lf_row = _log_sigmoid(jnp.concatenate([gr_ref[b][off + nh:off + 2 * nh, :] for b in range(nb)], axis=0))
    b_row = jnp.dot(lf_row, tri_row, precision=HIGHEST, preferred_element_type=F32)
    b_col = b_row.T
    beta_col = li_col - b_col
    beta_row = li_row - b_row
    cmx = beta_col
    rowi = lax.broadcasted_iota(jnp.int32, (ch, nq), 0)
    k = 1
    while k < ch:
        if reverse:
            cmx = jnp.maximum(cmx, jnp.where(rowi < ch - k, pltpu.roll(cmx, ch - k, 0), -jnp.inf))
        else:
            cmx = jnp.maximum(cmx, jnp.where(rowi >= k, pltpu.roll(cmx, k, 0), -jnp.inf))
        k *= 2
    last = 0 if reverse else ch - 1
    g_row = b_col[last:last + 1, :]
    m_loc = g_row + cmx[last:last + 1, :]
    m_prev = m_s[slot:slot + 1, 0:nq]
    mx = jnp.maximum(m_prev, cmx)
    s_inter = jnp.exp(m_prev - mx)
    inv_floor = jnp.exp(-mx - b_col)
    m_new = jnp.maximum(g_row + m_prev, m_loc)
    s_old = jnp.exp(g_row + m_prev - m_new)
    s_new = jnp.exp(m_loc - m_new)
    w_row = jnp.exp(beta_row - jnp.max(beta_row, axis=1, keepdims=True))
    m_s[slot:slot + 1, 0:nq] = m_new

    ones_col = (lax.broadcasted_iota(jnp.int32, (ch, HEAD_DIM), 1) == 0).astype(BF16)
    for b in range(nb):
        for h in range(nh):
            i = b * nh + h
            hs = slice(h * HEAD_DIM, (h + 1) * HEAD_DIM)
            q = q_ref[b, :, hs]
            kt = kt_ref[b, hs, :]
            v_aug = jnp.concatenate([v_ref[b, :, hs], ones_col], axis=1)
            d_mat = jnp.exp(jnp.where(mask, beta_row[i:i + 1, :] - mx[:, i:i + 1], -jnp.inf))
            scores = (_dot(q, kt) * d_mat).astype(BF16)
            c_prev = c_s[slot * nq + i]
            num = _dot(scores, v_aug) + s_inter[:, i:i + 1] * _dot(q, c_prev.astype(BF16))
            den = num[:, HEAD_DIM:HEAD_DIM + 1]
            out_ref[b, :, hs] = num[:, :HEAD_DIM] * (1.0 / jnp.maximum(jnp.abs(den), inv_floor[:, i:i + 1]))
            ktw = (kt.astype(F32) * w_row[i:i + 1, :]).astype(BF16)
            c_s[slot * nq + i] = s_old[:, i:i + 1] * c_prev + s_new[:, i:i + 1] * _dot(ktw, v_aug)


def _mlstm_kernel(qf, ktf, vf, gcf, grf, qb, ktb, vb, gcb, grb, of, ob, c_s, m_s):
    @pl.when(pl.program_id(0) == 0)
    def _():
        c_s[...] = jnp.zeros_like(c_s)
        m_s[...] = jnp.zeros_like(m_s)

    _mlstm_dir(False, qf, ktf, vf, gcf, grf, of, c_s, m_s, 0)
    _mlstm_dir(True, qb, ktb, vb, gcb, grb, ob, c_s, m_s, 1)


def _mlstm(q, kt, v, g_col, g_row):
    B, L, dm = q.shape
    ch = min(CHUNK, L)
    nc = L // ch
    specs = []
    for cidx in ((lambda c: c), (lambda c: nc - 1 - c)):
        tok = lambda w, cidx=cidx: pl.BlockSpec((B, ch, w), lambda c: (0, cidx(c), 0))
        tr = lambda w, cidx=cidx: pl.BlockSpec((B, w, ch), lambda c: (0, 0, cidx(c)))
        specs.append(([tok(dm), tr(dm), tok(dm), tok(g_col.shape[2]), tr(g_row.shape[1])], tok(dm)))
    args = [q, kt, v, g_col, g_row]
    out = jax.ShapeDtypeStruct((B, L, dm), F32)
    return pl.pallas_call(
        _mlstm_kernel, grid=(nc,), in_specs=specs[0][0] + specs[1][0],
        out_specs=(specs[0][1], specs[1][1]), out_shape=(out, out),
        scratch_shapes=[pltpu.VMEM((2 * B * MLSTM_HEADS, HEAD_DIM, 2 * HEAD_DIM), F32),
                        pltpu.VMEM((SUBLANES, 128), F32)],
        compiler_params=_params("arbitrary"), name="mlstm")(*args, *args)


N_FILTER_REFS = 12


def _filter_init(seq_len, refs, l1_ref, cb_s, sb_s):
    half = refs[0].shape[1] // 2
    hid = FILTER_HIDDEN
    grp = lax.broadcasted_iota(jnp.int32, (half, 2 * hid), 1) >= hid
    r2 = lax.broadcasted_iota(jnp.int32, (half, 2 * hid), 0)
    l1_ref[...] = jnp.zeros_like(l1_ref)
    lag0 = DFT_N1 * jnp.where(grp, half - r2, r2)
    ang = refs[1][...] * (lag0.astype(F32) * ((2.0 * math.pi) / seq_len))
    cb_s[...] = jnp.cos(ang)
    sb_s[...] = jnp.sin(ang)


def _filter_pass(seq_len, refs, n1_first, gs, store, l1_ref, cb_s, sb_s):
    (m_ref, fvec_ref, ph_ref, w1_ref, b1_ref, w2_ref, b2_ref, w3_ref, b3_ref, fr_ref, wf_ref,
     dec_ref) = refs
    half = m_ref.shape[1] // 2
    hid = FILTER_HIDDEN
    C = dec_ref.shape[1]
    L = seq_len
    lane = lax.broadcasted_iota(jnp.int32, (half, 2 * hid), 1)
    grp = lane >= hid
    feat = lane & (hid - 1)
    r2 = lax.broadcasted_iota(jnp.int32, (half, 2 * hid), 0)
    r_col = lax.broadcasted_iota(jnp.int32, (half, 1), 0)
    fvec = fvec_ref[...]
    fr = fr_ref[...]
    sign = jnp.where(grp[0:1], -1.0, 1.0)
    scale = (2.0 * math.pi) / L

    def lag(n):
        return jnp.where(n < L, n, jnp.where(n == L, 0, 2 * L - n)).astype(F32)

    grp_s = jnp.concatenate([grp] * gs, axis=0)
    zs = []
    for s in range(gs):
        n1 = n1_first + s
        n = n1 + DFT_N1 * (r2 + jnp.where(grp, half, 0))
        ang1 = sign * (fvec * (n1.astype(F32) * scale)) + ph_ref[...]
        feats = jnp.cos(ang1) * cb_s[...] - jnp.sin(ang1) * sb_s[...]
        feats = jnp.where(n == L, jnp.cos(ph_ref[...]), feats)
        zs.append(jnp.where(feat == 0, lag(n) / (L - 1),
                            jnp.where(feat <= 2 * HYENA_BANDS, feats, 0.0)))
    z = jnp.concatenate(zs, axis=0)
    h = jnp.sin(fr * (_dot(z.astype(BF16), w1_ref[...]) + b1_ref[...]))
    yield
    h = jnp.sin(fr * (_dot(h.astype(BF16), w2_ref[...]) + b2_ref[...]))
    yield
    h = jnp.sin(fr * (_dot(h.astype(BF16), w3_ref[...]) + b3_ref[...]))
    yield
    taps = []
    for gi in range(2):
        cs = slice(gi * C, (gi + 1) * C)
        hg = jnp.where(grp_s == (gi == 1), h, 0.0)
        tp = _dot(hg.astype(BF16), wf_ref[:, cs])
        dec = jnp.abs(dec_ref[gi:gi + 1, :])
        parts = []
        for s in range(gs):
            n_col = n1_first + s + DFT_N1 * (r_col + gi * half)
            tps = tp[s * half:(s + 1) * half] * jnp.exp(-(lag(n_col) / (L - 1)) * dec)
            l1_ref[...] += jnp.sum(jnp.abs(tps), axis=0, keepdims=True)
            parts.append(jnp.where(n_col == L, 0.0, tps).astype(BF16))
        taps.append(parts)
    for s in range(gs):
        store(s, _pack(_dot(m_ref[...], jnp.concatenate([taps[0][s], taps[1][s]], axis=0))))


def _blockdiag2(w):
    z = jnp.zeros_like(w)
    return jnp.concatenate([jnp.concatenate([w, z], axis=1), jnp.concatenate([z, w], axis=1)], axis=0)


def _filter_args(m1, w1, b1, w2, b2, w3, b3, fr, wfo, dec):
    hid = FILTER_HIDDEN
    fgrid = jnp.linspace(1e-4, HYENA_BANDS - 1, HYENA_BANDS, dtype=F32)
    fhalf = jnp.zeros((hid,), F32).at[1:1 + HYENA_BANDS].set(fgrid).at[1 + HYENA_BANDS:1 + 2 * HYENA_BANDS].set(fgrid)
    phalf = jnp.zeros((hid,), F32).at[1 + HYENA_BANDS:1 + 2 * HYENA_BANDS].set(0.5 * math.pi)
    two = lambda a: jnp.concatenate([a.reshape(1, -1), a.reshape(1, -1)], axis=1)
    w1p = jnp.zeros((hid, hid), F32).at[:w1.shape[0]].set(w1)
    mats = [_blockdiag2(wmat).astype(BF16) for wmat in (w1p, w2, w3)]
    wst = jnp.concatenate([wfo, wfo], axis=0).astype(BF16)
    args = (m1, two(fhalf), two(phalf), mats[0], two(b1), mats[1], two(b2), mats[2], two(b3), two(fr),
            wst, dec)
    assert len(args) == N_FILTER_REFS
    return args


def _filter_out(n2, C):
    shapes = (jax.ShapeDtypeStruct((DFT_N1, n2, C), jnp.int32), jax.ShapeDtypeStruct((1, C), F32))
    scratch = [pltpu.VMEM((n2 // 2, 2 * FILTER_HIDDEN), F32), pltpu.VMEM((n2 // 2, 2 * FILTER_HIDDEN), F32)]
    return shapes, scratch


@functools.lru_cache(maxsize=None)
def _dft_tables(n1, n2):
    n = n1 * n2
    h = n2 // 2
    a2 = 2.0 * np.pi * np.outer(np.arange(n2), np.arange(n2)) / n2
    c2, s2 = np.cos(a2), np.sin(a2)
    m1d = np.empty((n2, 2, n2))
    m1d[:, 0, :h], m1d[:, 0, h:] = c2[:, :h], s2[:, :h]
    m1d[:, 1, :h], m1d[:, 1, h:] = -s2[:, :h], c2[:, :h]
    m1f = np.stack([c2, -s2], axis=1)
    a1 = 2.0 * np.pi * np.outer(np.arange(n1), np.arange(n1)) / n1
    f1r, f1i = np.cos(a1), -np.sin(a1)
    il = lambda m: m.reshape(m.shape[0], 2, -1).swapaxes(1, 2).reshape(m.shape[0], -1)
    pg = il(np.block([[f1r, -f1i], [f1i, f1r]]))
    pgs = il(np.block([[-f1i, -f1r], [f1r, -f1i]]))
    qd = il(np.block([[f1r, f1i], [-f1i, f1r]]).T).T
    at = 2.0 * np.pi * np.outer(np.arange(n2), np.arange(n1)) / n
    tr, ti = np.cos(at), -np.sin(at)
    txr = np.repeat(tr, 2, axis=1)
    txi = np.repeat(ti, 2, axis=1)
    er, ei = c2[:h] / n, s2[:h] / n
    pp = il(np.block([[er, -ei], [ei, er]]))
    pps = il(np.block([[-ei, -er], [er, -ei]]))
    ur, ui = np.cos(at).T, np.sin(at).T
    uxr = np.repeat(ur, 2, axis=1)
    uxi = np.repeat(ui, 2, axis=1)
    f32 = lambda a: np.asarray(a, np.float32)
    return dict(m1d=f32(m1d.reshape(2 * n2, n2)), m1f=f32(m1f.reshape(2 * n2, n2)),
                pg=f32(pg), pgs=f32(pgs), qd=f32(qd), txr=f32(txr), txi=f32(txi),
                pp=f32(pp), pps=f32(pps), uxr=f32(uxr), uxi=f32(uxi))


def _pack(x):
    return pltpu.bitcast(x.astype(BF16), jnp.int32)


def _unpack(w):
    return pltpu.bitcast(w, BF16)


def _dft1_kernel(m_ref, x_ref, out_ref):
    nt = out_ref.shape[0]
    m = m_ref[...]
    for j in range(nt):
        parts = [x_ref[b, :, j, :] for b in range(x_ref.shape[0])]
        x = parts[0] if len(parts) == 1 else jnp.concatenate(parts, axis=0)
        out_ref[j] = _pack(_dot(m, x.astype(BF16)))


def _dft1(m1, x):
    parts, rows, n1, C = x.shape
    nt = SUBLANES
    return pl.pallas_call(
        _dft1_kernel, grid=(n1 // nt,),
        in_specs=[pl.BlockSpec(m1.shape, lambda i: (0, 0)),
                  pl.BlockSpec((parts, rows, nt, C), lambda i: (0, 0, i, 0))],
        out_specs=pl.BlockSpec((nt, m1.shape[0] // 2, C), lambda i: (i, 0, 0)),
        out_shape=jax.ShapeDtypeStruct((n1, m1.shape[0] // 2, C), jnp.int32),
        compiler_params=_params("parallel"), name="dft_stage1")(m1, x)


def _conv_mid_kernel(a_ref, af_ref, l1_ref, pg_ref, pgs_ref, txr_ref, txi_ref, qd_ref, out_ref):
    n1 = a_ref.shape[0]
    qd = qd_ref[...]
    inv = 1.0 / l1_ref[...]
    for j in range(out_ref.shape[0]):
        gd = (pg_ref[...] * txr_ref[j:j + 1, :] + pgs_ref[...] * txi_ref[j:j + 1, :]).astype(BF16)
        x = _dot(gd, _unpack(a_ref[:, j, :]))
        kf = _dot(gd, _unpack(af_ref[:, j, :])) * inv
        xr, xi = x[:n1], x[n1:]
        kr, ki = kf[:n1], kf[n1:]
        y = jnp.concatenate([xr * kr - xi * ki, xr * ki + xi * kr], axis=0).astype(BF16)
        out_ref[j] = _pack(_dot(qd, y))


def _conv_mid(a, af, l1, tabs):
    n1, n2, C = a.shape
    kt = SUBLANES
    ablk = pl.BlockSpec((n1, kt, C), lambda i: (0, i, 0))
    mat = pl.BlockSpec((2 * n1, 2 * n1), lambda i: (0, 0))
    tw = pl.BlockSpec((kt, 2 * n1), lambda i: (i, 0))
    return pl.pallas_call(
        _conv_mid_kernel, grid=(n2 // kt,),
        in_specs=[ablk, ablk, pl.BlockSpec((1, C), lambda i: (0, 0)), mat, mat, tw, tw, mat],
        out_specs=pl.BlockSpec((kt, n1, C), lambda i: (i, 0, 0)),
        out_shape=jax.ShapeDtypeStruct((n2, n1, C), jnp.int32),
        compiler_params=_params("parallel"), name="conv_mid")(
            a, af, l1, tabs["pg"], tabs["pgs"], tabs["txr"], tabs["txi"], tabs["qd"])


def _conv_out_kernel(b_ref, pp_ref, pps_ref, uxr_ref, uxi_ref, out_ref):
    h = out_ref.shape[1]
    for j in range(out_ref.shape[2]):
        hm = (pp_ref[...] * uxr_ref[j:j + 1, :] + pps_ref[...] * uxi_ref[j:j + 1, :]).astype(BF16)
        y = _dot(hm, _unpack(b_ref[:, j, :]))
        out_ref[0, :, j, :] = y[:h]
        out_ref[1, :, j, :] = y[h:]


def _conv_out(bm, tabs):
    n2, n1, C = bm.shape
    nt = SUBLANES
    mat = pl.BlockSpec((n2, 2 * n2), lambda i: (0, 0))
    tw = pl.BlockSpec((nt, 2 * n2), lambda i: (i, 0))
    return pl.pallas_call(
        _conv_out_kernel, grid=(n1 // nt,),
        in_specs=[pl.BlockSpec((n2, nt, C), lambda i: (0, i, 0)), mat, mat, tw, tw],
        out_specs=pl.BlockSpec((2, n2 // 2, nt, C), lambda i: (0, 0, i, 0)),
        out_shape=jax.ShapeDtypeStruct((2, n2 // 2, n1, C), F32),
        compiler_params=_params("parallel"), name="conv_out")(
            bm, tabs["pp"], tabs["pps"], tabs["uxr"], tabs["uxi"])


def _long_conv(zc, af, l1):
    B, L, C = zc.shape
    assert B == 2, "the two batch rows are packed as one complex sequence"
    n1 = DFT_N1
    n2 = 2 * L // n1
    tabs = {k: jnp.asarray(v) for k, v in _dft_tables(n1, n2).items()}
    a = _dft1(tabs["m1d"].astype(BF16), zc.reshape(B, n2 // 2, n1, C))
    bm = _conv_mid(a, af, l1, tabs)
    return _conv_out(bm, tabs).reshape(B, L, C)


def _merge_kernel(x_ref, hf_ref, hb_ref, o_ref, y_ref, zc_ref, x0_ref, hn_ref, skip_ref, g_ref,
                  wga_ref, bga_ref, wgb_ref, bgb_ref, wa_ref, wb_ref, wo_ref, out_ref):
    x = x_ref[0]
    n = _rms(x, g_ref[...]).astype(BF16)
    ga = jax.nn.sigmoid(_dot(n, wga_ref[...]) + bga_ref[...])
    gb = jax.nn.sigmoid(_dot(n, wgb_ref[...]) + bgb_ref[...])
    heads = []
    for h in range(MLSTM_HEADS):
        hs = slice(h * HEAD_DIM, (h + 1) * HEAD_DIM)
        s = hf_ref[0, :, hs] + hb_ref[0, :, hs]
        dv = s - jnp.mean(s, axis=1, keepdims=True)
        var = jnp.mean(dv * dv, axis=1, keepdims=True)
        heads.append((dv * lax.rsqrt(var + EPS) * hn_ref[:, hs]
                      * jax.nn.sigmoid(o_ref[0, :, hs])).astype(BF16))
    ml = jnp.concatenate(heads, axis=1)
    hy = (x0_ref[0] * (y_ref[0] + skip_ref[...] * zc_ref[0])).astype(BF16)
    merged = ga * _dot(ml, wa_ref[...]) + gb * _dot(hy, wb_ref[...])
    out_ref[0] = x + _dot(merged.astype(BF16), wo_ref[...])


def _merge(x, hf, hb, o, y, zc, x0, hn, skip, g, wga, bga, wgb, bgb, wa, wb, wo):
    B, L, D = x.shape
    tm = min(TOKEN_BLOCK, L)
    full = lambda w: pl.BlockSpec(w.shape, lambda bb, i: (0,) * w.ndim)
    tok = lambda w: pl.BlockSpec((1, tm, w), lambda bb, i: (bb, i, 0))
    ws = (hn, skip, g, wga, bga, wgb, bgb, wa, wb, wo)
    acts = (hf, hb, o, y, zc, x0)
    return pl.pallas_call(
        _merge_kernel, grid=(B, L // tm),
        in_specs=[tok(D)] + [tok(a.shape[2]) for a in acts] + [full(w) for w in ws],
        out_specs=tok(D), out_shape=jax.ShapeDtypeStruct(x.shape, F32),
        compiler_params=_params("parallel", "parallel"), name="merge")(x, *acts, *ws)


def _mlp_kernel(final, nchunk, host_len, x_ref, g_ref, w1_ref, w2_ref, gf_ref, *rest):
    out_ref = rest[N_FILTER_REFS] if host_len else rest[0]
    if host_len:
        af_ref, l1_ref, cb_s, sb_s = rest[N_FILTER_REFS + 1:]
        step = pl.program_id(0) * pl.num_programs(1) + pl.program_id(1)

        @pl.when(step == 0)
        def _():
            _filter_init(host_len, rest, l1_ref, cb_s, sb_s)

        gs = af_ref.shape[0]

        def store(s, val):
            af_ref[s] = val
        hosted = _filter_pass(host_len, rest[:N_FILTER_REFS], step * gs, gs, store, l1_ref, cb_s, sb_s)
    else:
        hosted = iter(())

    x = x_ref[0]
    n = _rms(x, g_ref[...]).astype(BF16)
    fc = w1_ref.shape[1] // nchunk
    acc = x
    for c in range(nchunk):
        next(hosted, None)
        hcl = jnp.maximum(_dot(n, w1_ref[:, c * fc:(c + 1) * fc]), 0.0)
        acc = acc + _dot((hcl * hcl).astype(BF16), w2_ref[c * fc:(c + 1) * fc, :])
    for _ in hosted:
        pass
    out_ref[0] = _rms(acc, gf_ref[...]) if final else acc


def _mlp(x, g, w1, w2, gf, final, host=None):
    B, L, D = x.shape
    tm = min(TOKEN_BLOCK, L)
    steps = B * (L // tm)
    full = lambda w: pl.BlockSpec(w.shape, lambda bb, i: (0,) * w.ndim)
    once = lambda w: pl.BlockSpec(w.shape, lambda bb, i: (0,) * w.ndim, pipeline_mode=pl.Buffered(1))
    tok = pl.BlockSpec((1, tm, D), lambda bb, i: (bb, i, 0))
    args = [x, g, w1, w2, gf]
    in_specs = [tok, full(g), once(w1), once(w2), full(gf)]
    out_specs, out_shape, scratch = tok, jax.ShapeDtypeStruct(x.shape, F32), []
    sem = ("parallel", "parallel")
    host_len = 0
    if host is not None:
        host_len, fargs = host
        n2 = fargs[0].shape[1]
        C = fargs[-1].shape[1]
        gs = DFT_N1 // steps
        assert gs * steps == DFT_N1
        args += list(fargs)
        in_specs += [full(a) for a in fargs]
        per_l = L // tm
        fshape, scratch = _filter_out(n2, C)
        out_specs = (tok, pl.BlockSpec((gs, n2, C), lambda bb, i: (bb * per_l + i, 0, 0)),
                     pl.BlockSpec((1, C), lambda bb, i: (0, 0)))
        out_shape = (out_shape,) + fshape
        sem = ("arbitrary", "arbitrary")
    return pl.pallas_call(
        functools.partial(_mlp_kernel, final, 4, host_len), grid=(B, L // tm),
        in_specs=in_specs, out_specs=out_specs, out_shape=out_shape, scratch_shapes=scratch,
        compiler_params=_params(*sem), name="mlp")(*args)


def _layer(x, p, l, filt, host):
    depth = p["w_in"].shape[0]
    D = x.shape[2]
    dm = HEAD_DIM * MLSTM_HEADS
    ng = 4 * MLSTM_HEADS
    dh = p["hy_skip"].shape[1]
    o_g, o_u, o_ga = 4 * dm, 4 * dm + ng, 4 * dm + ng + 3 * dh
    row = lambda a: a.reshape(1, -1)
    col = lambda a: a.reshape(-1, 1)
    w_in = p["w_in"][l]
    b_in = p["b_in"][l]
    w_qvo = jnp.concatenate([w_in[:, :dm], w_in[:, 2 * dm:4 * dm]], axis=1)
    b_qvo = jnp.concatenate([b_in[:dm], b_in[2 * dm:4 * dm]])
    own = None if filt is not None else (x.shape[1], _filter_operands(p, l, x.shape[1]))
    outs = _inproj(
        x, row(p["norm_mix"][l]), w_qvo.astype(BF16), row(b_qvo),
        w_in[:, dm:2 * dm].T.astype(BF16), col(b_in[dm:2 * dm]),
        w_in[:, o_g:o_u].T.astype(BF16), col(b_in[o_g:o_u]),
        w_in[:, o_u:o_ga].astype(BF16), row(b_in[o_u:o_ga]),
        p["hy_conv_w"][l], row(p["hy_conv_b"][l]), own)
    q, kt, v, o, g_col, g_row, x0, zc = outs[:8]
    if own is not None:
        filt = outs[8:]
    h_fwd, h_bwd = _mlstm(q, kt, v, g_col, g_row)
    y = _long_conv(zc, *filt)
    x = _merge(x, h_fwd, h_bwd, o, y, zc, x0, row(p["mlstm_norm"][l]), row(p["hy_skip"][l]),
               row(p["norm_mix"][l]),
               w_in[:, o_ga:o_ga + D].astype(BF16), row(b_in[o_ga:o_ga + D]),
               w_in[:, o_ga + D:].astype(BF16), row(b_in[o_ga + D:]),
               p["w_branch_a"][l].astype(BF16), p["w_branch_b"][l].astype(BF16),
               p["w_out"][l].astype(BF16))
    out = _mlp(x, row(p["norm_mlp"][l]), p["w_mlp1"][l].astype(BF16), p["w_mlp2"][l].astype(BF16),
               row(p["norm_final"]), l == depth - 1, host)
    return (out[0], out[1:]) if host is not None else (out, None)


def _filter_operands(p, l, L):
    n2 = 2 * L // DFT_N1
    m1f = jnp.asarray(_dft_tables(DFT_N1, n2)["m1f"]).astype(BF16)
    return _filter_args(m1f, p["hy_w1"][l], p["hy_b1"][l], p["hy_w2"][l], p["hy_b2"][l], p["hy_w3"][l],
                        p["hy_b3"][l], p["hy_freq"][l], p["hy_w_fo"][l], p["hy_decay"][l])


def _run(xs, p):
    depth = p["w_in"].shape[0]
    order = sorted(range(len(xs)), key=lambda gi: xs[gi].shape[1])
    blocks = [(gi, l) for l in range(depth) for gi in order]
    xs = list(xs)
    filt = None
    for bi, (gi, l) in enumerate(blocks):
        host = None
        if bi + 1 < len(blocks):
            gn, ln = blocks[bi + 1]
            host = (xs[gn].shape[1], _filter_operands(p, ln, xs[gn].shape[1]))
        xs[gi], filt = _layer(xs[gi], p, l, filt, host)
    return tuple(xs)


def kernel(x_prompt, x_sample, norm_mix, w_in, b_in, mlstm_norm, hy_conv_w, hy_conv_b, hy_w1, hy_b1, hy_w2, hy_b2, hy_w3, hy_b3, hy_freq, hy_w_fo, hy_decay, hy_skip, w_branch_a, w_branch_b, w_out, norm_mlp, w_mlp1, w_mlp2, norm_final):
    p = dict(norm_mix=norm_mix, w_in=w_in, b_in=b_in, mlstm_norm=mlstm_norm, hy_conv_w=hy_conv_w,
             hy_conv_b=hy_conv_b, hy_w1=hy_w1, hy_b1=hy_b1, hy_w2=hy_w2, hy_b2=hy_b2, hy_w3=hy_w3,
             hy_b3=hy_b3, hy_freq=hy_freq, hy_w_fo=hy_w_fo, hy_decay=hy_decay, hy_skip=hy_skip,
             w_branch_a=w_branch_a, w_branch_b=w_branch_b, w_out=w_out, norm_mlp=norm_mlp,
             w_mlp1=w_mlp1, w_mlp2=w_mlp2, norm_final=norm_final)
    return _run((x_prompt, x_sample), p)
```
